```python
import jax
import jax.numpy as jnp
from jax import lax
import numpy as np


D_MODEL = 2048
BATCH = 2
SEQ = 4096
DEPTH = 4
DEC_BATCH = 8
DEC_SEQ = 1
PAST_LEN = 16384
PAGE_SIZE = 128

H_ATT = 8
HD = 128
MOBA_BLOCK = 256
MOBA_TOPK = 3
Q_CHUNK = 16
H_GLA = 4
DK = 128
DV = 256
GATE_RANK = 16
GATE_TAU = 16.0
GLA_CHUNK = 64
ATT_W = H_ATT * HD
GLA_KW = H_GLA * DK
GLA_VW = H_GLA * DV
MIX_W = ATT_W + GLA_VW
N_IN = 3 * ATT_W + 2 * GLA_KW + 2 * GLA_VW + GATE_RANK
PEER_HEADS = 8
N_KEYS = 128
N_EXPERTS = N_KEYS * N_KEYS
PEER_TOPK = 16
PEER_QDIM = 256
PEER_BLOCK = 64
EPS = 1e-6

kernel_name = 'hymba_moba_gla_peer_step'


def rmsnorm(x, g):
    xf = x.astype(jnp.float32)
    y = xf * lax.rsqrt(jnp.mean(xf * xf, axis=-1, keepdims=True) + EPS)
    return (y * g.astype(jnp.float32)).astype(x.dtype)


def alibi_slopes():
    return 2.0 ** (-(8.0 / H_ATT) * jnp.arange(1, H_ATT + 1, dtype=jnp.float32))


def moba_attention(q, k, v, q_pos):
    B, H, Q, _ = q.shape
    L = k.shape[2]
    nb = -(-L // MOBA_BLOCK)
    pad = nb * MOBA_BLOCK - L
    k = jnp.pad(k, ((0, 0), (0, 0), (0, pad), (0, 0)))
    v = jnp.pad(v, ((0, 0), (0, 0), (0, pad), (0, 0)))
    k_blk = k.reshape(B, H, nb, MOBA_BLOCK, HD)
    v_blk = v.reshape(B, H, nb, MOBA_BLOCK, HD)
    k_mean = jnp.mean(k_blk.astype(jnp.float32), axis=3)
    n_sel = min(MOBA_TOPK, nb)
    slopes = alibi_slopes()
    qc = min(Q_CHUNK, Q)
    nq = -(-Q // qc)
    qpad = nq * qc - Q
    q_ch = jnp.pad(q, ((0, 0), (0, 0), (0, qpad), (0, 0))).reshape(B, H, nq, qc, HD).transpose(2, 0, 1, 3, 4)
    pos_ch = jnp.pad(q_pos, (0, qpad)).reshape(nq, qc)
    bi = jnp.arange(B)[:, None, None, None]
    hi = jnp.arange(H)[None, :, None, None]
    offs = jnp.arange(MOBA_BLOCK, dtype=jnp.int32)

    def one_chunk(args):
        qb, pb = args
        cur = pb // MOBA_BLOCK
        gate = jnp.einsum('bhqd,bhnd->bhqn', qb.astype(jnp.float32), k_mean)
        past = jnp.arange(nb)[None, :] < cur[:, None]
        gate = jnp.where(past, gate, -jnp.inf)
        _, sel = lax.top_k(gate, n_sel)
        own = jnp.broadcast_to(cur[None, None, :, None], (B, H, qc, 1)).astype(sel.dtype)
        ids = jnp.concatenate([sel, own], axis=-1)
        kg = k_blk[bi, hi, ids]
        vg = v_blk[bi, hi, ids]
        s = jnp.einsum('bhqd,bhqnkd->bhqnk', qb, kg).astype(jnp.float32) * (HD ** -0.5)
        kpos = ids[..., None] * MOBA_BLOCK + offs
        dist = (pb[None, None, :, None, None] - kpos).astype(jnp.float32)
        slot_ok = jnp.concatenate([jnp.arange(n_sel)[None, :] < cur[:, None],
                                   jnp.ones((qc, 1), dtype=bool)], axis=-1)
        mask = slot_ok[None, None, :, :, None] & (dist >= 0)
        s = jnp.where(mask, s - slopes[None, :, None, None, None] * dist, -jnp.inf)
        p = jax.nn.softmax(s.reshape(B, H, qc, -1), axis=-1).reshape(s.shape)
        return jnp.einsum('bhqnk,bhqnkd->bhqd', p.astype(vg.dtype), vg)

    out = lax.map(one_chunk, (q_ch, pos_ch))
    return out.transpose(1, 2, 0, 3, 4).reshape(B, H, nq * qc, HD)[:, :, :Q]


def gla_chunked(q, k, v, log_a, s0):
    B, L, H, _ = q.shape
    c = min(GLA_CHUNK, L)
    n = -(-L // c)
    pad = n * c - L

    def to_chunks(t):
        t = jnp.pad(t, ((0, 0), (0, pad), (0, 0), (0, 0)))
        return t.reshape(B, n, c, H, t.shape[-1]).transpose(1, 0, 3, 2, 4).astype(jnp.float32)

    causal = jnp.tril(jnp.ones((c, c), dtype=bool))

    def step(S, inp):
        qc_, kc, vc, ac = inp
        b = jnp.cumsum(ac, axis=2)
        o_inter = jnp.einsum('bhik,bhkv->bhiv', qc_ * jnp.exp(b), S)
        diff = b[:, :, :, None, :] - b[:, :, None, :, :]
        decay = jnp.exp(jnp.where(causal[None, None, :, :, None], diff, -jnp.inf))
        A = jnp.einsum('bhik,bhjk,bhijk->bhij', qc_, kc, decay)
        o = o_inter + jnp.einsum('bhij,bhjv->bhiv', A, vc)
        b_last = b[:, :, -1:, :]
        S_new = jnp.exp(b_last[:, :, 0, :])[..., None] * S + jnp.einsum('bhjk,bhjv->bhkv', kc * jnp.exp(b_last - b), vc)
        return S_new, o

    S, o = lax.scan(step, s0.astype(jnp.float32), (to_chunks(q), to_chunks(k), to_chunks(v), to_chunks(log_a)))
    o = o.transpose(1, 0, 3, 2, 4).reshape(B, n * c, H, DV)[:, :L]
    return o, S


def split_proj(proj):
    sizes = (ATT_W, ATT_W, ATT_W, GLA_KW, GLA_KW, GLA_VW, GLA_VW, GATE_RANK)
    idx = []
    acc = 0
    for s in sizes[:-1]:
        acc += s
        idx.append(acc)
    return jnp.split(proj, idx, axis=-1)


def token_mixers(xn, pos, k_past, v_past, s0, w_in, w_gate2, b_gate, g_att, g_gla, w_out):
    B, L, _ = xn.shape
    q_a, k_a, v_a, q_g, k_g, v_g, r_g, z_g = split_proj(xn @ w_in)
    k_new = k_a.reshape(B, L, H_ATT, HD)
    v_new = v_a.reshape(B, L, H_ATT, HD)
    k_all = k_new.transpose(0, 2, 1, 3)
    v_all = v_new.transpose(0, 2, 1, 3)
    if k_past is not None:
        k_all = jnp.concatenate([k_past.astype(k_all.dtype), k_all], axis=2)
        v_all = jnp.concatenate([v_past.astype(v_all.dtype), v_all], axis=2)
    o_a = moba_attention(q_a.reshape(B, L, H_ATT, HD).transpose(0, 2, 1, 3), k_all, v_all, pos)
    o_a = rmsnorm(o_a.transpose(0, 2, 1, 3), g_att.reshape(H_ATT, HD))
    log_a = jax.nn.log_sigmoid((z_g @ w_gate2 + b_gate).astype(jnp.float32)) / GATE_TAU
    o_g, s_fin = gla_chunked(q_g.reshape(B, L, H_GLA, DK) * (DK ** -0.5), k_g.reshape(B, L, H_GLA, DK),
                             v_g.reshape(B, L, H_GLA, DV), log_a.reshape(B, L, H_GLA, DK), s0)
    o_g = rmsnorm(o_g, g_gla.reshape(H_GLA, DV)) * jax.nn.silu(r_g.astype(jnp.float32)).reshape(B, L, H_GLA, DV)
    mix = jnp.concatenate([o_a.reshape(B, L, ATT_W).astype(xn.dtype),
                           o_g.reshape(B, L, GLA_VW).astype(xn.dtype)], axis=-1)
    return mix @ w_out, k_new, v_new, s_fin


def peer_ffn(xn, w_pq, sub_keys, u, v):
    B, L, D = xn.shape
    T = B * L
    xt = xn.reshape(T, D)
    q = (xt @ w_pq).reshape(T, PEER_HEADS, 2, PEER_QDIM // 2)
    s = jnp.einsum('thpc,hpnc->thpn', q, sub_keys).astype(jnp.float32)
    sv, si = lax.top_k(s, PEER_TOPK)
    cand = (sv[:, :, 0, :, None] + sv[:, :, 1, None, :]).reshape(T, PEER_HEADS, PEER_TOPK * PEER_TOPK)
    cand_idx = (si[:, :, 0, :, None] * N_KEYS + si[:, :, 1, None, :]).reshape(T, PEER_HEADS, PEER_TOPK * PEER_TOPK)
    top_s, top_c = lax.top_k(cand, PEER_TOPK)
    idx = jnp.take_along_axis(cand_idx, top_c, axis=-1).reshape(T, PEER_HEADS * PEER_TOPK)
    gate = jax.nn.softmax(top_s, axis=-1).reshape(T, PEER_HEADS * PEER_TOPK)
    tb = min(PEER_BLOCK, T)
    nt = -(-T // tb)
    pad = nt * tb - T
    xb = jnp.pad(xt, ((0, pad), (0, 0))).reshape(nt, tb, D)
    ib = jnp.pad(idx, ((0, pad), (0, 0))).reshape(nt, tb, -1)
    gb = jnp.pad(gate, ((0, pad), (0, 0))).reshape(nt, tb, -1)

    def blk(args):
        xx, ii, gg = args
        hdn = jax.nn.gelu(jnp.einsum('td,tkd->tk', xx, u[ii]).astype(jnp.float32), approximate=False)
        return jnp.einsum('tk,tkd->td', (gg * hdn).astype(xx.dtype), v[ii])

    y = lax.map(blk, (xb, ib, gb)).reshape(nt * tb, D)[:T]
    return y.reshape(B, L, D)


def setup_inputs(seed: int = 0) -> dict:
    key = jax.random.key(seed)
    ks = jax.random.split(key, 20)
    f32 = jnp.float32
    n_pages = PAST_LEN // PAGE_SIZE
    n_used = DEC_BATCH * n_pages
    n_pool = n_used + (n_used + 3) // 4

    def nrm(k, shape, scale):
        return jax.random.normal(k, shape, f32) * scale

    def gain(k, shape):
        return 1.0 + 0.01 * jax.random.normal(k, shape, f32)

    page_table = jax.random.permutation(ks[5], n_pool)[:n_used].reshape(DEC_BATCH, n_pages).astype(jnp.int32)
    return {
        'x_prompt': nrm(ks[0], (BATCH, SEQ, D_MODEL), 1.0),
        'x_sample': nrm(ks[1], (DEC_BATCH, DEC_SEQ, D_MODEL), 1.0),
        'cache_k': nrm(ks[2], (DEPTH, n_pool, PAGE_SIZE, H_ATT, HD), 1.0),
        'cache_v': nrm(ks[3], (DEPTH, n_pool, PAGE_SIZE, H_ATT, HD), 1.0),
        'state_gla': nrm(ks[4], (DEPTH, DEC_BATCH, H_GLA, DK, DV), 0.5),
        'page_table': page_table,
        'norm1_g': gain(ks[6], (DEPTH, D_MODEL)),
        'w_in': nrm(ks[7], (DEPTH, D_MODEL, N_IN), D_MODEL ** -0.5),
        'w_gate2': nrm(ks[8], (DEPTH, GATE_RANK, GLA_KW), GATE_RANK ** -0.5),
        'b_gate': nrm(ks[9], (DEPTH, GLA_KW), 0.1),
        'att_norm_g': gain(ks[10], (DEPTH, ATT_W)),
        'gla_norm_g': gain(ks[11], (DEPTH, GLA_VW)),
        'w_out': nrm(ks[12], (DEPTH, MIX_W, D_MODEL), MIX_W ** -0.5),
        'norm2_g': gain(ks[13], (DEPTH, D_MODEL)),
        'w_pq': nrm(ks[14], (DEPTH, D_MODEL, PEER_HEADS * PEER_QDIM), D_MODEL ** -0.5),
        'peer_sub_keys': nrm(ks[15], (DEPTH, PEER_HEADS, 2, N_KEYS, PEER_QDIM // 2), (PEER_QDIM // 2) ** -0.5),
        'peer_u': nrm(ks[16], (DEPTH, N_EXPERTS, D_MODEL), D_MODEL ** -0.5),
        'peer_v': nrm(ks[17], (DEPTH, N_EXPERTS, D_MODEL), (PEER_HEADS * PEER_TOPK) ** -0.5),
        'final_norm_g': gain(ks[18], (D_MODEL,)),
    }


def reference(x_prompt, x_sample, cache_k, cache_v, state_gla, page_table, norm1_g, w_in, w_gate2, b_gate,
              att_norm_g, gla_norm_g, w_out, norm2_g, w_pq, peer_sub_keys, peer_u, peer_v, final_norm_g):
    Bp, S, _ = x_prompt.shape
    Bd, Sd, _ = x_sample.shape
    n_pages = page_table.shape[1]
    past_len = n_pages * cache_k.shape[2]
    pos_p = jnp.arange(S, dtype=jnp.int32)
    pos_d = past_len + jnp.arange(Sd, dtype=jnp.int32)
    s0_p = jnp.zeros((Bp, H_GLA, DK, DV), jnp.float32)
    hp, hd = x_prompt, x_sample
    kp_l, vp_l, sp_l, kd_l, vd_l, sd_l = [], [], [], [], [], []
    for l in range(DEPTH):
        mix, kp, vp, sp = token_mixers(rmsnorm(hp, norm1_g[l]), pos_p, None, None, s0_p, w_in[l], w_gate2[l],
                                       b_gate[l], att_norm_g[l], gla_norm_g[l], w_out[l])
        hp = hp + mix
        hp = hp + peer_ffn(rmsnorm(hp, norm2_g[l]), w_pq[l], peer_sub_keys[l], peer_u[l], peer_v[l])
        k_past = cache_k[l][page_table].reshape(Bd, past_len, H_ATT, HD).transpose(0, 2, 1, 3)
        v_past = cache_v[l][page_table].reshape(Bd, past_len, H_ATT, HD).transpose(0, 2, 1, 3)
        mix, kd, vd, sd = token_mixers(rmsnorm(hd, norm1_g[l]), pos_d, k_past, v_past, state_gla[l], w_in[l],
                                       w_gate2[l], b_gate[l], att_norm_g[l], gla_norm_g[l], w_out[l])
        hd = hd + mix
        hd = hd + peer_ffn(rmsnorm(hd, norm2_g[l]), w_pq[l], peer_sub_keys[l], peer_u[l], peer_v[l])
        kp_l.append(kp)
        vp_l.append(vp)
        sp_l.append(sp)
        kd_l.append(kd)
        vd_l.append(vd)
        sd_l.append(sd)
    y_prompt = rmsnorm(hp, final_norm_g)
    y_sample = rmsnorm(hd, final_norm_g)
    k_prompt = jnp.stack(kp_l)
    v_prompt = jnp.stack(vp_l)
    gla_prompt = jnp.stack(sp_l)
    k_sample = jnp.stack(kd_l)
    v_sample = jnp.stack(vd_l)
    gla_sample = jnp.stack(sd_l)
    return (y_prompt, y_sample, k_prompt, v_prompt, gla_prompt, k_sample, v_sample, gla_sample)
```

```python
import functools

import jax
import jax.numpy as jnp
from jax import lax
from jax.experimental import pallas as pl
from jax.experimental.pallas import tpu as pltpu

F32 = jnp.float32
BF16 = jnp.bfloat16
HI = lax.Precision.HIGHEST

LANES = 128
SUBLANES = 8
VMEM_LIMIT_BYTES = 56 * 1024 * 1024

D_MODEL = 2048
H_ATT = 8
HD = 128
MOBA_BLOCK = 256
MOBA_TOPK = 3
H_GLA = 4
DK = 128
DV = 256
GATE_RANK = 16
GATE_TAU = 16.0
GLA_CHUNK = 64
ATT_W = H_ATT * HD
GLA_KW = H_GLA * DK
GLA_VW = H_GLA * DV
N_IN = 3 * ATT_W + 2 * GLA_KW + 2 * GLA_VW + GATE_RANK
N_IN_PAD = N_IN - GATE_RANK + LANES
PEER_HEADS = 8
N_KEYS = 128
N_EXPERTS = N_KEYS * N_KEYS
PEER_TOPK = 16
PEER_QDIM = 256
EPS = 1e-6

CB_QA = 0
CB_KA = ATT_W // LANES
CB_VA = 2 * ATT_W // LANES
CB_QG = 3 * ATT_W // LANES
CB_KG = CB_QG + GLA_KW // LANES
CB_VG = (3 * ATT_W + 2 * GLA_KW) // DV
CB_RG = CB_VG + GLA_VW // DV
CB_Z = (N_IN - GATE_RANK) // LANES

SAMPLE_PAD = 128
EXPERT_TILE = SUBLANES * N_KEYS


def _cparams(*sem):
    return pltpu.CompilerParams(dimension_semantics=sem, vmem_limit_bytes=VMEM_LIMIT_BYTES)


def _rms(x, g):
    return x * lax.rsqrt(jnp.mean(x * x, axis=-1, keepdims=True) + EPS) * g


def _mm(a, b):
    return jnp.dot(a, b, precision=HI if a.dtype == F32 else None, preferred_element_type=F32)


def _dot_nt(a, b, precision=None):
    return lax.dot_general(a, b, (((1,), (1,)), ((), ())), precision=precision,
                           preferred_element_type=F32)


def _norm_matmul_kernel(*refs, has_y):
    if has_y:
        h_ref, yt_ref, g_ref, w_ref, x_out_ref, o_ref, xn_scr = refs
    else:
        x_ref, g_ref, w_ref, o_ref, xn_scr = refs

    @pl.when(pl.program_id(1) == 0)
    def _():
        if has_y:
            x = h_ref[...] + yt_ref[...].T
            x_out_ref[...] = x
        else:
            x = x_ref[...]
        xn_scr[...] = _rms(x, g_ref[...]).astype(xn_scr.dtype)

    o_ref[...] = _mm(xn_scr[...], w_ref[...])


def _norm_matmul(x, yt, g, w, tm, tn):
    t, d = x.shape
    n = w.shape[1]
    has_y = yt is not None
    in_specs = [pl.BlockSpec((tm, d), lambda i, j: (i, 0))]
    args = [x]
    if has_y:
        in_specs.append(pl.BlockSpec((d, tm), lambda i, j: (0, i)))
        args.append(yt)
    in_specs += [pl.BlockSpec((1, d), lambda i, j: (0, 0)),
                 pl.BlockSpec((d, tn), lambda i, j: (0, j))]
    args += [g.reshape(1, d), w]
    out_shape = [jax.ShapeDtypeStruct((t, n), F32)]
    out_specs = [pl.BlockSpec((tm, tn), lambda i, j: (i, j))]
    if has_y:
        out_shape.insert(0, jax.ShapeDtypeStruct((t, d), F32))
        out_specs.insert(0, pl.BlockSpec((tm, d), lambda i, j: (i, 0)))
    res = pl.pallas_call(
        functools.partial(_norm_matmul_kernel, has_y=has_y),
        grid=(t // tm, n // tn),
        in_specs=in_specs, out_specs=out_specs, out_shape=out_shape,
        scratch_shapes=[pltpu.VMEM((tm, d), w.dtype)],
        compiler_params=_cparams("parallel", "arbitrary"),
        name="norm_in_proj",
    )(*args)
    if has_y:
        return res[0], res[1]
    return x, res[0]


def _final_norm_kernel(h_ref, yt_ref, g_ref, o_ref):
    o_ref[...] = _rms(h_ref[...] + yt_ref[...].T, g_ref[...])


def _final_norm(h, yt, g, tm):
    t, d = h.shape
    return pl.pallas_call(
        _final_norm_kernel,
        grid=(t // tm,),
        in_specs=[pl.BlockSpec((tm, d), lambda i: (i, 0)),
                  pl.BlockSpec((d, tm), lambda i: (0, i)),
                  pl.BlockSpec((1, d), lambda i: (0, 0))],
        out_specs=pl.BlockSpec((tm, d), lambda i: (i, 0)),
        out_shape=jax.ShapeDtypeStruct((t, d), F32),
        compiler_params=_cparams("parallel"),
        name="final_norm",
    )(h, yt, g.reshape(1, d))


def _moba_prompt_kernel(slope_ref, q_ref, k_ref, v_ref, g_ref, o_ref,
                        kb_scr, vb_scr, kmean_scr, *, nb):
    h = pl.program_id(1)
    i = pl.program_id(2)
    blk = MOBA_BLOCK

    @pl.when(i == 0)
    def _():
        k = k_ref[...]
        kb_scr[...] = k.astype(BF16)
        vb_scr[...] = v_ref[...].astype(BF16)
        kmean_scr[...] = jnp.mean(k.reshape(nb, blk, HD), axis=1)

    slope = slope_ref[h]
    scale = HD ** -0.5
    q = q_ref[...]
    qb = q.astype(BF16)

    gate = _dot_nt(q, kmean_scr[...], HI)
    nidx = lax.broadcasted_iota(jnp.int32, (blk, nb), 1)
    nidx_f = nidx.astype(F32)
    gate = jnp.where(nidx < i, gate, -jnp.inf)
    sel = jnp.zeros((blk, nb), F32)
    for s in range(MOBA_TOPK):
        m = jnp.max(gate, axis=1, keepdims=True)
        first = jnp.min(jnp.where(gate == m, nidx_f, float(nb)), axis=1, keepdims=True)
        pick = jnp.logical_and(nidx_f == first, jnp.full((blk, nb), i, jnp.int32) > s)
        sel = jnp.where(pick, 1.0, sel)
        gate = jnp.where(pick, -jnp.inf, gate)

    rr = lax.broadcasted_iota(jnp.int32, (blk, blk), 0)
    cc = lax.broadcasted_iota(jnp.int32, (blk, blk), 1)
    rel_i = rr - cc
    rel = rel_i.astype(F32)

    row0 = pl.multiple_of(i * blk, blk)
    s_own = _dot_nt(qb, kb_scr[pl.ds(row0, blk), :]) * scale - slope * rel
    s_own = jnp.where(rel >= 0, s_own, -jnp.inf)
    m0 = jnp.max(s_own, axis=1, keepdims=True)
    p0 = jnp.exp(s_own - m0)
    l0 = jnp.sum(p0, axis=1, keepdims=True)
    acc0 = jnp.dot(p0.astype(BF16), vb_scr[pl.ds(row0, blk), :], preferred_element_type=F32)

    def body(j, carry):
        m, l, acc = carry
        rj = pl.multiple_of(j * blk, blk)
        use = jnp.sum(jnp.where(nidx == j, sel, 0.0), axis=1, keepdims=True)
        dist = (rel_i + (i - j) * blk).astype(F32)
        s = _dot_nt(qb, kb_scr[pl.ds(rj, blk), :]) * scale - slope * dist
        s = jnp.where(use > 0.0, s, -jnp.inf)
        m_new = jnp.maximum(m, jnp.max(s, axis=1, keepdims=True))
        alpha = jnp.exp(m - m_new)
        p = jnp.exp(s - m_new)
        l = alpha * l + jnp.sum(p, axis=1, keepdims=True)
        acc = alpha * acc + jnp.dot(p.astype(BF16), vb_scr[pl.ds(rj, blk), :],
                                    preferred_element_type=F32)
        return m_new, l, acc

    _, l, acc = lax.fori_loop(0, i, body, (m0, l0, acc0))
    o = acc / l
    o_ref[...] = _rms(o, g_ref[...]).astype(o_ref.dtype)


def _moba_prompt(proj, slopes, g_att, bsz, seq):
    nb = seq // MOBA_BLOCK
    return pl.pallas_call(
        functools.partial(_moba_prompt_kernel, nb=nb),
        grid_spec=pltpu.PrefetchScalarGridSpec(
            num_scalar_prefetch=0,
            grid=(bsz, H_ATT, nb),
            in_specs=[
                pl.BlockSpec(memory_space=pltpu.SMEM),
                pl.BlockSpec((MOBA_BLOCK, HD), lambda b, h, i: (b * nb + i, CB_QA + h)),
                pl.BlockSpec((seq, HD), lambda b, h, i: (b, CB_KA + h)),
                pl.BlockSpec((seq, HD), lambda b, h, i: (b, CB_VA + h)),
                pl.BlockSpec((1, HD), lambda b, h, i: (0, h)),
            ],
            out_specs=pl.BlockSpec((MOBA_BLOCK, HD), lambda b, h, i: (b * nb + i, h)),
            scratch_shapes=[pltpu.VMEM((seq, HD), BF16), pltpu.VMEM((seq, HD), BF16),
                            pltpu.VMEM((nb, HD), F32)],
        ),
        out_shape=jax.ShapeDtypeStruct((bsz * seq, ATT_W), BF16),
        compiler_params=_cparams("parallel", "parallel", "arbitrary"),
        name="moba_prompt",
    )(slopes, proj, proj, proj, g_att.reshape(1, ATT_W))


def _log_decay(z, w2, bg):
    x = jnp.dot(z, w2, precision=HI, preferred_element_type=F32) + bg
    return (jnp.minimum(x, 0.0) - jnp.log1p(jnp.exp(-jnp.abs(x)))) * (1.0 / GATE_TAU)


def _gla_prompt_kernel(q_ref, k_ref, v_ref, r_ref, z_ref, w2_ref, bg_ref, g_ref,
                       o_ref, s_out_ref, s_scr, b_scr, q_scr, oi_scr, *, nchunk):
    c = GLA_CHUNK
    s_scr[...] = jnp.zeros_like(s_scr)
    ri = lax.broadcasted_iota(jnp.int32, (c, c), 0)
    ci = lax.broadcasted_iota(jnp.int32, (c, c), 1)
    tril = (ri >= ci).astype(F32)
    eye = lax.broadcasted_iota(jnp.int32, (DK, DK), 0) == lax.broadcasted_iota(jnp.int32, (DK, DK), 1)

    def chunk(n, _):
        r0 = pl.multiple_of(n * c, c)
        rows = pl.ds(r0, c)
        a = _log_decay(z_ref[rows, :], w2_ref[...], bg_ref[...])
        b = jnp.dot(tril, a, precision=HI, preferred_element_type=F32)
        q = q_ref[rows, :] * (DK ** -0.5)
        k = k_ref[rows, :]
        v = v_ref[rows, :]
        s_old = s_scr[...]
        o_inter = jnp.dot(q * jnp.exp(b), s_old, precision=HI, preferred_element_type=F32)
        b_scr[...] = b
        q_scr[...] = q
        oi_scr[...] = o_inter
        for j in range(c):
            g0 = (j // SUBLANES) * SUBLANES
            nr = c - g0
            bj = b_scr[j:j + 1, :]
            kj = k_ref[pl.ds(r0 + j, 1), :]
            vj = v_ref[pl.ds(r0 + j, 1), :]
            rid = lax.broadcasted_iota(jnp.int32, (nr, DK), 0) + g0
            d = jnp.where(rid >= j, b_scr[g0:c, :] - bj, -jnp.inf)
            w = jnp.exp(d) * (q_scr[g0:c, :] * kj)
            aj = jnp.sum(w, axis=1, keepdims=True)
            oi_scr[g0:c, :] = oi_scr[g0:c, :] + aj * vj
        b_last = b[c - 1:c, :]
        e_last = jnp.exp(b_last)
        e_col = jnp.sum(jnp.where(eye, jnp.broadcast_to(e_last, (DK, DK)), 0.0), axis=1, keepdims=True)
        kd = k * jnp.exp(b_last - b)
        s_scr[...] = e_col * s_old + jnp.dot(kd.T, v, precision=HI, preferred_element_type=F32)
        o = oi_scr[...]
        rg = r_ref[rows, :]
        o_ref[rows, :] = (_rms(o, g_ref[...]) * (rg * jax.nn.sigmoid(rg))).astype(o_ref.dtype)
        return 0

    lax.fori_loop(0, nchunk, chunk, 0)
    s_out_ref[...] = s_scr[...]


def _gla_prompt(proj, w2p, bg, g_gla, bsz, seq):
    nchunk = seq // GLA_CHUNK
    return pl.pallas_call(
        functools.partial(_gla_prompt_kernel, nchunk=nchunk),
        grid=(bsz, H_GLA),
        in_specs=[
            pl.BlockSpec((seq, DK), lambda b, h: (b, CB_QG + h)),
            pl.BlockSpec((seq, DK), lambda b, h: (b, CB_KG + h)),
            pl.BlockSpec((seq, DV), lambda b, h: (b, CB_VG + h)),
            pl.BlockSpec((seq, DV), lambda b, h: (b, CB_RG + h)),
            pl.BlockSpec((seq, LANES), lambda b, h: (b, CB_Z)),
            pl.BlockSpec((LANES, DK), lambda b, h: (0, h)),
            pl.BlockSpec((1, DK), lambda b, h: (0, h)),
            pl.BlockSpec((1, DV), lambda b, h: (0, h)),
        ],
        out_specs=[
            pl.BlockSpec((seq, DV), lambda b, h: (b, h)),
            pl.BlockSpec((None, None, DK, DV), lambda b, h: (b, h, 0, 0)),
        ],
        out_shape=[jax.ShapeDtypeStruct((bsz * seq, GLA_VW), BF16),
                   jax.ShapeDtypeStruct((bsz, H_GLA, DK, DV), F32)],
        scratch_shapes=[pltpu.VMEM((DK, DV), F32), pltpu.VMEM((GLA_CHUNK, DK), F32),
                        pltpu.VMEM((GLA_CHUNK, DK), F32), pltpu.VMEM((GLA_CHUNK, DV), F32)],
        compiler_params=_cparams("parallel", "parallel"),
        name="gla_prompt",
    )(proj, proj, proj, proj, proj, w2p, bg.reshape(1, GLA_KW), g_gla.reshape(1, GLA_VW))


def _page_sum_kernel(pt_ref, p0_ref, p1_ref, o_ref):
    o_ref[...] = (jnp.sum(p0_ref[...], axis=0, keepdims=True)
                  + jnp.sum(p1_ref[...], axis=0, keepdims=True))


def _moba_block_sums(cache_k4, pt_flat, layer, bsz, n_pages):
    page, width = cache_k4.shape[2], cache_k4.shape[3]
    nblk = n_pages * page // MOBA_BLOCK
    ppb = MOBA_BLOCK // page
    assert ppb == 2

    def idx(which):
        return lambda b, n, pt: (layer, pt[b * n_pages + ppb * n + which], 0, 0)

    return pl.pallas_call(
        _page_sum_kernel,
        grid_spec=pltpu.PrefetchScalarGridSpec(
            num_scalar_prefetch=1,
            grid=(bsz, nblk),
            in_specs=[pl.BlockSpec((None, None, page, width), idx(0)),
                      pl.BlockSpec((None, None, page, width), idx(1))],
            out_specs=pl.BlockSpec((None, None, 1, width), lambda b, n, pt: (b, n, 0, 0)),
        ),
        out_shape=jax.ShapeDtypeStruct((bsz, nblk, 1, width), F32),
        compiler_params=_cparams("parallel", "arbitrary"),
        name="moba_block_sums",
    )(pt_flat, cache_k4, cache_k4)


def _moba_select_kernel(q_ref, ks_ref, seg_ref, sel_ref, *, nblk):
    b = pl.program_id(0)
    q = q_ref[pl.ds(b, 1), :]
    prod = ks_ref[...] * q
    gate = jnp.dot(prod, seg_ref[...], precision=HI, preferred_element_type=F32) * (1.0 / MOBA_BLOCK)
    ridx = lax.broadcasted_iota(jnp.int32, (nblk, LANES), 0).astype(F32)
    rows = []
    for _ in range(MOBA_TOPK):
        m = jnp.max(gate, axis=0, keepdims=True)
        first = jnp.min(jnp.where(gate == m, ridx, float(nblk)), axis=0, keepdims=True)
        rows.append(first)
        gate = jnp.where(ridx == first, -jnp.inf, gate)
    rows += [jnp.zeros((1, LANES), F32)] * (SUBLANES - MOBA_TOPK)
    sel_ref[...] = jnp.concatenate(rows, axis=0).astype(jnp.int32)


def _moba_select(proj_d, ksum, bsz):
    nblk = ksum.shape[1]
    seg = (jnp.arange(ATT_W)[:, None] // HD == jnp.arange(LANES)[None, :]).astype(F32)
    return pl.pallas_call(
        functools.partial(_moba_select_kernel, nblk=nblk),
        grid=(bsz,),
        in_specs=[pl.BlockSpec((SUBLANES, ATT_W), lambda b: (0, 0)),
                  pl.BlockSpec((None, nblk, ATT_W), lambda b: (b, 0, 0)),
                  pl.BlockSpec((ATT_W, LANES), lambda b: (0, 0))],
        out_specs=pl.BlockSpec((None, SUBLANES, LANES), lambda b: (b, 0, 0)),
        out_shape=jax.ShapeDtypeStruct((bsz, SUBLANES, LANES), jnp.int32),
        compiler_params=_cparams("parallel"),
        name="moba_select",
    )(proj_d, ksum, seg)


def _moba_sample_kernel(sel_ref, pt_ref, slope_ref, q_ref, kn_ref, vn_ref, kp_ref, vp_ref, g_ref,
                        o_ref, m_scr, l_scr, acc_scr, *, bsz, past_len, page, ppb):
    h = pl.program_id(0)
    b = pl.program_id(1)
    s = pl.program_id(2)
    pp = pl.program_id(3)
    scale = HD ** -0.5
    slope = slope_ref[h]
    q = q_ref[...]

    @pl.when(jnp.logical_and(s == 0, pp == 0))
    def _():
        m_scr[...] = jnp.sum(q * kn_ref[...], axis=1, keepdims=True) * scale
        l_scr[...] = jnp.ones_like(l_scr)
        acc_scr[...] = vn_ref[...]

    blk_id = sel_ref[(b * H_ATT + h) * MOBA_TOPK + s]
    kpos = blk_id * MOBA_BLOCK + pp * page + lax.broadcasted_iota(jnp.int32, (bsz, page), 1)
    dist = (past_len - kpos).astype(F32)
    sc = _dot_nt(q, kp_ref[...], HI) * scale - slope * dist
    m_old = m_scr[...]
    m_new = jnp.maximum(m_old, jnp.max(sc, axis=1, keepdims=True))
    alpha = jnp.exp(m_old - m_new)
    p = jnp.exp(sc - m_new)
    l_scr[...] = alpha * l_scr[...] + jnp.sum(p, axis=1, keepdims=True)
    acc_scr[...] = alpha * acc_scr[...] + jnp.dot(p, vp_ref[...], precision=HI,
                                                  preferred_element_type=F32)
    m_scr[...] = m_new

    @pl.when(jnp.logical_and(s == MOBA_TOPK - 1, pp == ppb - 1))
    def _():
        o = _rms(acc_scr[...] / l_scr[...], g_ref[...])
        live = lax.broadcasted_iota(jnp.int32, (bsz, HD), 0) == b
        o_ref[pl.ds(b, 1), :] = jnp.sum(jnp.where(live, o, 0.0), axis=0, keepdims=True)


def _moba_sample(proj_d, cache_k, cache_v, sel_flat, pt_flat, slopes, g_att, layer, bsz, n_pages):
    page = cache_k.shape[2]
    ppb = MOBA_BLOCK // page
    past_len = n_pages * page

    def page_idx(h, b, s, pp, sel, pt):
        return (layer, pt[b * n_pages + ppb * sel[(b * H_ATT + h) * MOBA_TOPK + s] + pp], 0, h)

    return pl.pallas_call(
        functools.partial(_moba_sample_kernel, bsz=bsz, past_len=past_len, page=page, ppb=ppb),
        grid_spec=pltpu.PrefetchScalarGridSpec(
            num_scalar_prefetch=2,
            grid=(H_ATT, bsz, MOBA_TOPK, ppb),
            in_specs=[
                pl.BlockSpec(memory_space=pltpu.SMEM),
                pl.BlockSpec((bsz, HD), lambda h, b, s, pp, sel, pt: (0, CB_QA + h)),
                pl.BlockSpec((bsz, HD), lambda h, b, s, pp, sel, pt: (0, CB_KA + h)),
                pl.BlockSpec((bsz, HD), lambda h, b, s, pp, sel, pt: (0, CB_VA + h)),
                pl.BlockSpec((None, None, page, HD), page_idx),
                pl.BlockSpec((None, None, page, HD), page_idx),
                pl.BlockSpec((1, HD), lambda h, b, s, pp, sel, pt: (0, h)),
            ],
            out_specs=pl.BlockSpec((bsz, HD), lambda h, b, s, pp, sel, pt: (0, h)),
            scratch_shapes=[pltpu.VMEM((bsz, 1), F32), pltpu.VMEM((bsz, 1), F32),
                            pltpu.VMEM((bsz, HD), F32)],
        ),
        out_shape=jax.ShapeDtypeStruct((bsz, ATT_W), F32),
        compiler_params=_cparams("arbitrary", "arbitrary", "arbitrary", "arbitrary"),
        name="moba_sample",
    )(sel_flat, pt_flat, slopes, proj_d, proj_d, proj_d, cache_k, cache_v, g_att.reshape(1, ATT_W))


def _gla_sample_kernel(q_ref, k_ref, v_ref, r_ref, z_ref, w2_ref, bg_ref, g_ref, s0_ref,
                       o_ref, s_out_ref, *, bsz):
    b = pl.program_id(1)
    a = _log_decay(z_ref[...], w2_ref[...], bg_ref[...])
    live = lax.broadcasted_iota(jnp.int32, (bsz, 1), 0) == b

    def row(x):
        return jnp.sum(jnp.where(live, x, 0.0), axis=0, keepdims=True)

    eye = lax.broadcasted_iota(jnp.int32, (DK, DK), 0) == lax.broadcasted_iota(jnp.int32, (DK, DK), 1)

    def col(x):
        return jnp.sum(jnp.where(eye, jnp.broadcast_to(x, (DK, DK)), 0.0), axis=1, keepdims=True)

    e = jnp.exp(row(a))
    q = row(q_ref[...]) * (DK ** -0.5)
    k = row(k_ref[...])
    v = row(v_ref[...])
    rg = row(r_ref[...])
    s0 = s0_ref[...]
    o = jnp.sum(col(q * e) * s0, axis=0, keepdims=True) + jnp.sum(q * k, axis=1, keepdims=True) * v
    s_out_ref[...] = col(e) * s0 + col(k) * v
    on = _rms(o, g_ref[...]) * (rg * jax.nn.sigmoid(rg))
    o_ref[pl.ds(b, 1), :] = on


def _gla_sample(proj_d, state5, w2p, bg, g_gla, layer, bsz):
    return pl.pallas_call(
        functools.partial(_gla_sample_kernel, bsz=bsz),
        grid=(H_GLA, bsz),
        in_specs=[
            pl.BlockSpec((bsz, DK), lambda h, b: (0, CB_QG + h)),
            pl.BlockSpec((bsz, DK), lambda h, b: (0, CB_KG + h)),
            pl.BlockSpec((bsz, DV), lambda h, b: (0, CB_VG + h)),
            pl.BlockSpec((bsz, DV), lambda h, b: (0, CB_RG + h)),
            pl.BlockSpec((bsz, LANES), lambda h, b: (0, CB_Z)),
            pl.BlockSpec((LANES, DK), lambda h, b: (0, h)),
            pl.BlockSpec((1, DK), lambda h, b: (0, h)),
            pl.BlockSpec((1, DV), lambda h, b: (0, h)),
            pl.BlockSpec((None, None, None, DK, DV), lambda h, b: (layer, b, h, 0, 0)),
        ],
        out_specs=[
            pl.BlockSpec((bsz, DV), lambda h, b: (0, h)),
            pl.BlockSpec((None, None, DK, DV), lambda h, b: (b, h, 0, 0)),
        ],
        out_shape=[jax.ShapeDtypeStruct((bsz, GLA_VW), F32),
                   jax.ShapeDtypeStruct((bsz, H_GLA, DK, DV), F32)],
        compiler_params=_cparams("arbitrary", "arbitrary"),
        name="gla_sample",
    )(proj_d, proj_d, proj_d, proj_d, proj_d, w2p, bg.reshape(1, GLA_KW), g_gla.reshape(1, GLA_VW),
      state5)


def _out_proj_kernel(oa_ref, og_ref, wa_ref, wg_ref, x_ref, g_ref, h_ref, xnt_ref):
    h = x_ref[...] + _mm(oa_ref[...], wa_ref[...]) + _mm(og_ref[...], wg_ref[...])
    h_ref[...] = h
    xnt_ref[...] = _rms(h, g_ref[...]).T.astype(xnt_ref.dtype)


def _out_proj(oa, og, wa, wg, x, g2, tm):
    t, d = x.shape
    return pl.pallas_call(
        _out_proj_kernel,
        grid=(t // tm,),
        in_specs=[pl.BlockSpec((tm, ATT_W), lambda i: (i, 0)),
                  pl.BlockSpec((tm, GLA_VW), lambda i: (i, 0)),
                  pl.BlockSpec((ATT_W, d), lambda i: (0, 0)),
                  pl.BlockSpec((GLA_VW, d), lambda i: (0, 0)),
                  pl.BlockSpec((tm, d), lambda i: (i, 0)),
                  pl.BlockSpec((1, d), lambda i: (0, 0))],
        out_specs=[pl.BlockSpec((tm, d), lambda i: (i, 0)),
                   pl.BlockSpec((d, tm), lambda i: (0, i))],
        out_shape=[jax.ShapeDtypeStruct((t, d), F32), jax.ShapeDtypeStruct((d, t), wa.dtype)],
        compiler_params=_cparams("parallel"),
        name="out_proj",
    )(oa, og, wa, wg, x, g2.reshape(1, d))


def _top_values(s, n):
    vals = []
    for _ in range(n):
        m = jnp.max(s, axis=0, keepdims=True)
        vals.append(m)
        s = jnp.where(s == m, -jnp.inf, s)
    return vals


def _peer_route_kernel(xnt_ref, wq_ref, sk_ref, s1_ref, s2_ref, e1_ref, e2_ref, tau_ref, q_scr):
    q_scr[...] = _mm(wq_ref[...], xnt_ref[...])
    half = PEER_QDIM // 2

    def head(h, _):
        r1 = pl.multiple_of(h * PEER_QDIM, PEER_QDIM)
        r2 = pl.multiple_of(h * PEER_QDIM + half, half)
        s1 = jnp.dot(sk_ref[h, 0], q_scr[pl.ds(r1, half), :], precision=HI, preferred_element_type=F32)
        s2 = jnp.dot(sk_ref[h, 1], q_scr[pl.ds(r2, half), :], precision=HI, preferred_element_type=F32)
        v1 = _top_values(s1, PEER_TOPK)
        v2 = _top_values(s2, PEER_TOPK)
        v2s = jnp.concatenate(v2, axis=0)
        cand = jnp.concatenate([v1[a] + v2s for a in range(PEER_TOPK)], axis=0)
        tau = _top_values(cand, PEER_TOPK)[-1]
        smax = v1[0] + v2[0]
        z = jnp.sum(jnp.where(cand >= tau, jnp.exp(cand - smax), 0.0), axis=0, keepdims=True)
        s1_ref[h] = s1
        s2_ref[h] = s2
        e1_ref[h] = jnp.exp(s1 - v1[0]) / z
        e2_ref[h] = jnp.exp(s2 - v2[0])
        tau_ref[pl.ds(h, 1), :] = tau
        return 0

    lax.fori_loop(0, PEER_HEADS, head, 0)


def _peer_route(xnt, wqt, sub_keys, tm):
    d, t = xnt.shape
    nq = wqt.shape[0]
    big = jax.ShapeDtypeStruct((PEER_HEADS, N_KEYS, t), F32)
    bspec = pl.BlockSpec((PEER_HEADS, N_KEYS, tm), lambda i: (0, 0, i))
    return pl.pallas_call(
        _peer_route_kernel,
        grid=(t // tm,),
        in_specs=[pl.BlockSpec((d, tm), lambda i: (0, i)),
                  pl.BlockSpec((nq, d), lambda i: (0, 0)),
                  pl.BlockSpec((PEER_HEADS, 2, N_KEYS, PEER_QDIM // 2), lambda i: (0, 0, 0, 0))],
        out_specs=[bspec, bspec, bspec, bspec, pl.BlockSpec((PEER_HEADS, tm), lambda i: (0, i))],
        out_shape=[big, big, big, big, jax.ShapeDtypeStruct((PEER_HEADS, t), F32)],
        scratch_shapes=[pltpu.VMEM((nq, tm), F32)],
        compiler_params=_cparams("parallel"),
        name="peer_route",
    )(xnt, wqt, sub_keys)


def _peer_dense_kernel(xnt_ref, u_ref, vt_ref, s1_ref, s2_ref, e1_ref, e2_ref, tau_ref,
                       yt_ref, p_scr, ht_scr, *, te, tm):
    j = pl.program_id(1)

    @pl.when(j == 0)
    def _():
        yt_ref[...] = jnp.zeros_like(yt_ref)

    ht_scr[...] = _mm(u_ref[...], xnt_ref[...])
    n1 = te // N_KEYS
    assert n1 == SUBLANES
    g1 = pl.ds(pl.multiple_of(j * n1, SUBLANES), n1)
    for il in range(n1):
        rs = slice(il * N_KEYS, (il + 1) * N_KEYS)
        for c in range(tm // LANES):
            cs = slice(c * LANES, (c + 1) * LANES)
            w = jnp.zeros((N_KEYS, LANES), F32)
            for h in range(PEER_HEADS):
                s1row = s1_ref[h, g1, cs][il:il + 1, :]
                e1row = e1_ref[h, g1, cs][il:il + 1, :]
                tot = s1row + s2_ref[h, :, cs]
                w = w + jnp.where(tot >= tau_ref[h:h + 1, cs], e1row * e2_ref[h, :, cs], 0.0)
            ht = ht_scr[rs, cs]
            act = 0.5 * ht * (1.0 + lax.erf(ht * (2.0 ** -0.5)))
            p_scr[rs, cs] = (w * act).astype(p_scr.dtype)
    yt_ref[...] += _mm(vt_ref[...], p_scr[...])


def _peer_dense(xnt, u_b, vt_b, s1, s2, e1, e2, tau, tm, te):
    d, t = xnt.shape
    ne = u_b.shape[0]
    rspec = pl.BlockSpec((PEER_HEADS, N_KEYS, tm), lambda i, j: (0, 0, i))
    return pl.pallas_call(
        functools.partial(_peer_dense_kernel, te=te, tm=tm),
        grid=(t // tm, ne // te),
        in_specs=[pl.BlockSpec((d, tm), lambda i, j: (0, i)),
                  pl.BlockSpec((te, d), lambda i, j: (j, 0)),
                  pl.BlockSpec((d, te), lambda i, j: (0, j)),
                  rspec, rspec, rspec, rspec,
                  pl.BlockSpec((PEER_HEADS, tm), lambda i, j: (0, i))],
        out_specs=pl.BlockSpec((d, tm), lambda i, j: (0, i)),
        out_shape=jax.ShapeDtypeStruct((d, t), F32),
        scratch_shapes=[pltpu.VMEM((te, tm), u_b.dtype), pltpu.VMEM((te, tm), F32)],
        compiler_params=_cparams("parallel", "arbitrary"),
        name="peer_dense",
    )(xnt, u_b, vt_b, s1, s2, e1, e2, tau)


def _token_tiles(t):
    if t >= 512:
        return 512, 256, 256, 512
    return t, t, t, t


def _layer_weights(l, norm1_g, w_in, w_gate2, b_gate, att_norm_g, gla_norm_g, w_out, norm2_g, w_pq,
                   peer_sub_keys, peer_u, peer_v):
    f32 = dict(
        g1=norm1_g[l],
        w_in=jnp.pad(w_in[l], ((0, 0), (0, N_IN_PAD - N_IN))),
        w2p=jnp.pad(w_gate2[l], ((0, LANES - GATE_RANK), (0, 0))),
        bg=b_gate[l], g_att=att_norm_g[l], g_gla=gla_norm_g[l],
        wo_a=w_out[l, :ATT_W], wo_g=w_out[l, ATT_W:],
        g2=norm2_g[l],
        wqt=w_pq[l].T,
        sub_keys=peer_sub_keys[l],
        u=peer_u[l],
        vt=peer_v[l].T,
    )
    bf16 = dict(f32)
    for name in ("w_in", "wo_a", "wo_g", "wqt", "u", "vt"):
        bf16[name] = f32[name].astype(BF16)
    return f32, bf16


def _peer(h, xnt, w, tm_route, tm_exp):
    route = _peer_route(xnt, w["wqt"], w["sub_keys"], tm_route)
    return _peer_dense(xnt, w["u"], w["vt"], *route, tm_exp, EXPERT_TILE)


def _prompt_layer(x, yt, w, slopes, bsz, seq):
    tm_in, tm_out, tm_route, tm_exp = _token_tiles(bsz * seq)
    x, proj = _norm_matmul(x, yt, w["g1"], w["w_in"], tm_in, 896)
    oa = _moba_prompt(proj, slopes, w["g_att"], bsz, seq)
    og, state = _gla_prompt(proj, w["w2p"], w["bg"], w["g_gla"], bsz, seq)
    h, xnt = _out_proj(oa, og, w["wo_a"], w["wo_g"], x, w["g2"], tm_out)
    return h, _peer(h, xnt, w, tm_route, tm_exp), proj, state


def _sample_layer(x, yt, w, slopes, cache_k4, cache_v4, state_gla, pt_flat, layer, bsz, n_pages):
    tm_in, tm_out, tm_route, tm_exp = _token_tiles(SAMPLE_PAD)
    x, proj = _norm_matmul(x, yt, w["g1"], w["w_in"], tm_in, 896)
    ksum = _moba_block_sums(cache_k4, pt_flat, layer, bsz, n_pages)
    ksum = ksum.reshape(bsz, ksum.shape[1], ATT_W)
    sel = _moba_select(proj, ksum, bsz)
    sel_flat = sel[:, :MOBA_TOPK, :H_ATT].transpose(0, 2, 1).reshape(-1)
    oa = _moba_sample(proj, cache_k4, cache_v4, sel_flat, pt_flat, slopes, w["g_att"], layer, bsz, n_pages)
    og, state = _gla_sample(proj, state_gla, w["w2p"], w["bg"], w["g_gla"], layer, bsz)
    oa = jnp.pad(oa, ((0, SAMPLE_PAD - bsz), (0, 0)))
    og = jnp.pad(og, ((0, SAMPLE_PAD - bsz), (0, 0)))
    h, xnt = _out_proj(oa, og, w["wo_a"], w["wo_g"], x, w["g2"], tm_out)
    return h, _peer(h, xnt, w, tm_route, tm_exp), proj, state


def kernel(x_prompt, x_sample, cache_k, cache_v, state_gla, page_table, norm1_g, w_in, w_gate2, b_gate,
           att_norm_g, gla_norm_g, w_out, norm2_g, w_pq, peer_sub_keys, peer_u, peer_v, final_norm_g):
    bp, seq, d = x_prompt.shape
    bd, sd, _ = x_sample.shape
    depth = w_in.shape[0]
    n_pages = page_table.shape[1]
    assert sd == 1 and d == D_MODEL and seq % MOBA_BLOCK == 0 and bd <= SUBLANES
    tp = bp * seq

    slopes = 2.0 ** (-(8.0 / H_ATT) * jnp.arange(1, H_ATT + 1, dtype=F32))
    pt_flat = page_table.reshape(-1).astype(jnp.int32)
    cache_k4 = cache_k.reshape(cache_k.shape[0], cache_k.shape[1], cache_k.shape[2], ATT_W)
    cache_v4 = cache_v.reshape(cache_k4.shape)

    xp = x_prompt.reshape(tp, d)
    xd = jnp.pad(x_sample.reshape(bd, d), ((0, SAMPLE_PAD - bd), (0, 0)))
    ytp = ytd = None
    kp_l, vp_l, sp_l, kd_l, vd_l, sd_l = [], [], [], [], [], []

    for l in range(depth):
        w_f32, w_bf16 = _layer_weights(l, norm1_g, w_in, w_gate2, b_gate, att_norm_g, gla_norm_g, w_out,
                                       norm2_g, w_pq, peer_sub_keys, peer_u, peer_v)
        xp, ytp, proj_p, s_p = _prompt_layer(xp, ytp, w_bf16, slopes, bp, seq)
        xd, ytd, proj_d, s_d = _sample_layer(xd, ytd, w_f32, slopes, cache_k4, cache_v4, state_gla, pt_flat,
                                             l, bd, n_pages)
        kp_l.append(proj_p[:, ATT_W:2 * ATT_W].reshape(bp, seq, H_ATT, HD))
        vp_l.append(proj_p[:, 2 * ATT_W:3 * ATT_W].reshape(bp, seq, H_ATT, HD))
        sp_l.append(s_p)
        kd_l.append(proj_d[:bd, ATT_W:2 * ATT_W].reshape(bd, 1, H_ATT, HD))
        vd_l.append(proj_d[:bd, 2 * ATT_W:3 * ATT_W].reshape(bd, 1, H_ATT, HD))
        sd_l.append(s_d)

    y_prompt = _final_norm(xp, ytp, final_norm_g, 512).reshape(bp, seq, d)
    y_sample = _final_norm(xd, ytd, final_norm_g, SAMPLE_PAD)[:bd].reshape(bd, 1, d)
    return (y_prompt, y_sample, jnp.stack(kp_l), jnp.stack(vp_l), jnp.stack(sp_l),
            jnp.stack(kd_l), jnp.stack(vd_l), jnp.stack(sd_l))
```

```python
import functools

import jax
import jax.numpy as jnp
from jax import lax
from jax.experimental import pallas as pl
from jax.experimental.pallas import tpu as pltpu

F32 = jnp.float32
BF16 = jnp.bfloat16
HI = lax.Precision.HIGHEST

LANES = 128
SUBLANES = 8
VMEM_LIMIT_BYTES = 56 * 1024 * 1024

D_MODEL = 2048
H_ATT = 8
HD = 128
MOBA_BLOCK = 256
MOBA_TOPK = 3
H_GLA = 4
DK = 128
DV = 256
GATE_RANK = 16
GATE_TAU = 16.0
GLA_CHUNK = 64
ATT_W = H_ATT * HD
GLA_KW = H_GLA * DK
GLA_VW = H_GLA * DV
N_IN = 3 * ATT_W + 2 * GLA_KW + 2 * GLA_VW + GATE_RANK
N_IN_PAD = N_IN - GATE_RANK + LANES
PEER_HEADS = 8
N_KEYS = 128
N_EXPERTS = N_KEYS * N_KEYS
PEER_TOPK = 16
PEER_QDIM = 256
EPS = 1e-6

CB_QA = 0
CB_KA = ATT_W // LANES
CB_VA = 2 * ATT_W // LANES
CB_QG = 3 * ATT_W // LANES
CB_KG = CB_QG + GLA_KW // LANES
CB_VG = (3 * ATT_W + 2 * GLA_KW) // DV
CB_RG = CB_VG + GLA_VW // DV
CB_Z = (N_IN - GATE_RANK) // LANES

SAMPLE_PAD = 128
EXPERT_TILE = SUBLANES * N_KEYS


def _cparams(*sem):
    return pltpu.CompilerParams(dimension_semantics=sem, vmem_limit_bytes=VMEM_LIMIT_BYTES)


def _rms(x, g):
    return x * lax.rsqrt(jnp.mean(x * x, axis=-1, keepdims=True) + EPS) * g


def _mm(a, b):
    return jnp.dot(a, b, precision=HI if a.dtype == F32 else None, preferred_element_type=F32)


def _dot_nt(a, b, precision=None):
    return lax.dot_general(a, b, (((1,), (1,)), ((), ())), precision=precision,
                           preferred_element_type=F32)


def _norm_matmul_kernel(*refs, has_y):
    if has_y:
        h_ref, yt_ref, g_ref, w_ref, x_out_ref, o_ref, xn_scr = refs
    else:
        x_ref, g_ref, w_ref, o_ref, xn_scr = refs

    @pl.when(pl.program_id(1) == 0)
    def _():
        if has_y:
            x = h_ref[...] + yt_ref[...].T
            x_out_ref[...] = x
        else:
            x = x_ref[...]
        xn_scr[...] = _rms(x, g_ref[...]).astype(xn_scr.dtype)

    o_ref[...] = _mm(xn_scr[...], w_ref[...])


def _norm_matmul(x, yt, g, w, tm, tn):
    t, d = x.shape
    n = w.shape[1]
    has_y = yt is not None
    in_specs = [pl.BlockSpec((tm, d), lambda i, j: (i, 0))]
    args = [x]
    if has_y:
        in_specs.append(pl.BlockSpec((d, tm), lambda i, j: (0, i)))
        args.append(yt)
    in_specs += [pl.BlockSpec((1, d), lambda i, j: (0, 0)),
                 pl.BlockSpec((d, tn), lambda i, j: (0, j))]
    args += [g.reshape(1, d), w]
    out_shape = [jax.ShapeDtypeStruct((t, n), F32)]
    out_specs = [pl.BlockSpec((tm, tn), lambda i, j: (i, j))]
    if has_y:
        out_shape.insert(0, jax.ShapeDtypeStruct((t, d), F32))
        out_specs.insert(0, pl.BlockSpec((tm, d), lambda i, j: (i, 0)))
    res = pl.pallas_call(
        functools.partial(_norm_matmul_kernel, has_y=has_y),
        grid=(t // tm, n // tn),
        in_specs=in_specs, out_specs=out_specs, out_shape=out_shape,
        scratch_shapes=[pltpu.VMEM((tm, d), w.dtype)],
        compiler_params=_cparams("parallel", "arbitrary"),
        name="norm_in_proj",
    )(*args)
    if has_y:
        return res[0], res[1]
    return x, res[0]


def _final_norm_kernel(h_ref, yt_ref, g_ref, o_ref):
    o_ref[...] = _rms(h_ref[...] + yt_ref[...].T, g_ref[...])


def _final_norm(h, yt, g, tm):
    t, d = h.shape
    return pl.pallas_call(
        _final_norm_kernel,
        grid=(t // tm,),
        in_specs=[pl.BlockSpec((tm, d), lambda i: (i, 0)),
                  pl.BlockSpec((d, tm), lambda i: (0, i)),
                  pl.BlockSpec((1, d), lambda i: (0, 0))],
        out_specs=pl.BlockSpec((tm, d), lambda i: (i, 0)),
        out_shape=jax.ShapeDtypeStruct((t, d), F32),
        compiler_params=_cparams("parallel"),
        name="final_norm",
    )(h, yt, g.reshape(1, d))


def _moba_prompt_kernel(slope_ref, q_ref, k_ref, v_ref, g_ref, o_ref,
                        kb_scr, vb_scr, kmean_scr, *, nb, group):
    h = pl.program_id(1)
    i = pl.program_id(2)
    blk = MOBA_BLOCK

    @pl.when(i == 0)
    def _():
        k = k_ref[...]
        kb_scr[...] = k.astype(BF16)
        vb_scr[...] = v_ref[...].astype(BF16)
        kmean_scr[...] = jnp.mean(k.reshape(nb, blk, HD), axis=1)

    slope = slope_ref[h]
    scale = HD ** -0.5
    q = q_ref[...]
    qb = q.astype(BF16)

    gate = _dot_nt(q, kmean_scr[...], HI)
    nidx = lax.broadcasted_iota(jnp.int32, (blk, nb), 1)
    nidx_f = nidx.astype(F32)
    gate = jnp.where(nidx < i, gate, -jnp.inf)
    sel = jnp.zeros((blk, nb), F32)
    for s in range(MOBA_TOPK):
        m = jnp.max(gate, axis=1, keepdims=True)
        first = jnp.min(jnp.where(gate == m, nidx_f, float(nb)), axis=1, keepdims=True)
        pick = jnp.logical_and(nidx_f == first, jnp.full((blk, nb), i, jnp.int32) > s)
        sel = jnp.where(pick, 1.0, sel)
        gate = jnp.where(pick, -jnp.inf, gate)

    rr = lax.broadcasted_iota(jnp.int32, (blk, blk), 0)
    cc = lax.broadcasted_iota(jnp.int32, (blk, blk), 1)
    rel_i = rr - cc
    rel = rel_i.astype(F32)

    row0 = pl.multiple_of(i * blk, blk)
    s_own = _dot_nt(qb, kb_scr[pl.ds(row0, blk), :]) * scale - slope * rel
    s_own = jnp.where(rel >= 0, s_own, -jnp.inf)
    m0 = jnp.max(s_own, axis=1, keepdims=True)
    p0 = jnp.exp(s_own - m0)
    l0 = jnp.sum(p0, axis=1, keepdims=True)
    acc0 = jnp.dot(p0.astype(BF16), vb_scr[pl.ds(row0, blk), :], preferred_element_type=F32)

    def body(g, carry):
        m, l, acc = carry
        s_parts = []
        for c in range(group):
            j = g * group + c
            rj = pl.multiple_of(j * blk, blk)
            use = jnp.sum(jnp.where(nidx == j, sel, 0.0), axis=1, keepdims=True)
            dist = (rel_i + (i - j) * blk).astype(F32)
            s = _dot_nt(qb, kb_scr[pl.ds(rj, blk), :]) * scale - slope * dist
            s_parts.append(jnp.where(use > 0.0, s, -jnp.inf))
        m_new = m
        for s in s_parts:
            m_new = jnp.maximum(m_new, jnp.max(s, axis=1, keepdims=True))
        alpha = jnp.exp(m - m_new)
        l = alpha * l
        acc = alpha * acc
        for c, s in enumerate(s_parts):
            rj = pl.multiple_of((g * group + c) * blk, blk)
            p = jnp.exp(s - m_new)
            l = l + jnp.sum(p, axis=1, keepdims=True)
            acc = acc + jnp.dot(p.astype(BF16), vb_scr[pl.ds(rj, blk), :], preferred_element_type=F32)
        return m_new, l, acc

    _, l, acc = lax.fori_loop(0, (i + group - 1) // group, body, (m0, l0, acc0))
    o = acc / l
    o_ref[...] = _rms(o, g_ref[...]).astype(o_ref.dtype)


def _moba_prompt(proj, slopes, g_att, bsz, seq):
    nb = seq // MOBA_BLOCK
    group = next(g for g in (4, 2, 1) if nb % g == 0)
    return pl.pallas_call(
        functools.partial(_moba_prompt_kernel, nb=nb, group=group),
        grid_spec=pltpu.PrefetchScalarGridSpec(
            num_scalar_prefetch=0,
            grid=(bsz, H_ATT, nb),
            in_specs=[
                pl.BlockSpec(memory_space=pltpu.SMEM),
                pl.BlockSpec((MOBA_BLOCK, HD), lambda b, h, i: (b * nb + i, CB_QA + h)),
                pl.BlockSpec((seq, HD), lambda b, h, i: (b, CB_KA + h)),
                pl.BlockSpec((seq, HD), lambda b, h, i: (b, CB_VA + h)),
                pl.BlockSpec((1, HD), lambda b, h, i: (0, h)),
            ],
            out_specs=pl.BlockSpec((MOBA_BLOCK, HD), lambda b, h, i: (b * nb + i, h)),
            scratch_shapes=[pltpu.VMEM((seq, HD), BF16), pltpu.VMEM((seq, HD), BF16),
                            pltpu.VMEM((nb, HD), F32)],
        ),
        out_shape=jax.ShapeDtypeStruct((bsz * seq, ATT_W), BF16),
        compiler_params=_cparams("parallel", "parallel", "arbitrary"),
        name="moba_prompt",
    )(slopes, proj, proj, proj, g_att.reshape(1, ATT_W))


def _log_decay(z, w2, bg):
    x = jnp.dot(z, w2, precision=HI, preferred_element_type=F32) + bg
    return (jnp.minimum(x, 0.0) - jnp.log1p(jnp.exp(-jnp.abs(x)))) * (1.0 / GATE_TAU)


def _gla_prompt_kernel(q_ref, k_ref, v_ref, r_ref, z_ref, w2_ref, bg_ref, g_ref,
                       o_ref, s_out_ref, s_scr, b_scr, q_scr, oi_scr, *, nchunk):
    c = GLA_CHUNK
    s_scr[...] = jnp.zeros_like(s_scr)
    ri = lax.broadcasted_iota(jnp.int32, (c, c), 0)
    ci = lax.broadcasted_iota(jnp.int32, (c, c), 1)
    tril = (ri >= ci).astype(F32)
    eye = lax.broadcasted_iota(jnp.int32, (DK, DK), 0) == lax.broadcasted_iota(jnp.int32, (DK, DK), 1)

    def chunk(n, _):
        r0 = pl.multiple_of(n * c, c)
        rows = pl.ds(r0, c)
        a = _log_decay(z_ref[rows, :], w2_ref[...], bg_ref[...])
        b = jnp.dot(tril, a, precision=HI, preferred_element_type=F32)
        q = q_ref[rows, :] * (DK ** -0.5)
        k = k_ref[rows, :]
        v = v_ref[rows, :]
        s_old = s_scr[...]
        o_inter = jnp.dot(q * jnp.exp(b), s_old, precision=HI, preferred_element_type=F32)
        b_scr[...] = b
        q_scr[...] = q
        oi_scr[...] = o_inter
        for j in range(c):
            g0 = (j // SUBLANES) * SUBLANES
            nr = c - g0
            bj = b_scr[j:j + 1, :]
            kj = k_ref[pl.ds(r0 + j, 1), :]
            vj = v_ref[pl.ds(r0 + j, 1), :]
            rid = lax.broadcasted_iota(jnp.int32, (nr, DK), 0) + g0
            d = jnp.where(rid >= j, b_scr[g0:c, :] - bj, -jnp.inf)
            w = jnp.exp(d) * (q_scr[g0:c, :] * kj)
            aj = jnp.sum(w, axis=1, keepdims=True)
            oi_scr[g0:c, :] = oi_scr[g0:c, :] + aj * vj
        b_last = b[c - 1:c, :]
        e_last = jnp.exp(b_last)
        e_col = jnp.sum(jnp.where(eye, jnp.broadcast_to(e_last, (DK, DK)), 0.0), axis=1, keepdims=True)
        kd = k * jnp.exp(b_last - b)
        s_scr[...] = e_col * s_old + jnp.dot(kd.T, v, precision=HI, preferred_element_type=F32)
        o = oi_scr[...]
        rg = r_ref[rows, :]
        o_ref[rows, :] = (_rms(o, g_ref[...]) * (rg * jax.nn.sigmoid(rg))).astype(o_ref.dtype)
        return 0

    lax.fori_loop(0, nchunk, chunk, 0)
    s_out_ref[...] = s_scr[...]


def _gla_prompt(proj, w2p, bg, g_gla, bsz, seq):
    nchunk = seq // GLA_CHUNK
    return pl.pallas_call(
        functools.partial(_gla_prompt_kernel, nchunk=nchunk),
        grid=(bsz, H_GLA),
        in_specs=[
            pl.BlockSpec((seq, DK), lambda b, h: (b, CB_QG + h)),
            pl.BlockSpec((seq, DK), lambda b, h: (b, CB_KG + h)),
            pl.BlockSpec((seq, DV), lambda b, h: (b, CB_VG + h)),
            pl.BlockSpec((seq, DV), lambda b, h: (b, CB_RG + h)),
            pl.BlockSpec((seq, LANES), lambda b, h: (b, CB_Z)),
            pl.BlockSpec((LANES, DK), lambda b, h: (0, h)),
            pl.BlockSpec((1, DK), lambda b, h: (0, h)),
            pl.BlockSpec((1, DV), lambda b, h: (0, h)),
        ],
        out_specs=[
            pl.BlockSpec((seq, DV), lambda b, h: (b, h)),
            pl.BlockSpec((None, None, DK, DV), lambda b, h: (b, h, 0, 0)),
        ],
        out_shape=[jax.ShapeDtypeStruct((bsz * seq, GLA_VW), BF16),
                   jax.ShapeDtypeStruct((bsz, H_GLA, DK, DV), F32)],
        scratch_shapes=[pltpu.VMEM((DK, DV), F32), pltpu.VMEM((GLA_CHUNK, DK), F32),
                        pltpu.VMEM((GLA_CHUNK, DK), F32), pltpu.VMEM((GLA_CHUNK, DV), F32)],
        compiler_params=_cparams("parallel", "parallel"),
        name="gla_prompt",
    )(proj, proj, proj, proj, proj, w2p, bg.reshape(1, GLA_KW), g_gla.reshape(1, GLA_VW))


SUM_PAGES = 8


def _page_sum_kernel(pt_ref, *refs, ppb):
    pages, o_ref = refs[:-1], refs[-1]
    for n in range(len(pages) // ppb):
        tot = jnp.sum(pages[n * ppb][...], axis=0)
        for r in range(1, ppb):
            tot = tot + jnp.sum(pages[n * ppb + r][...], axis=0)
        o_ref[n] = tot


def _moba_block_sums(cache_k, pt_flat, layer, bsz, n_pages):
    page = cache_k.shape[2]
    ppb = MOBA_BLOCK // page
    nblk = n_pages // ppb
    bps = SUM_PAGES // ppb
    assert nblk % bps == 0

    def idx(r):
        return lambda b, g, pt: (layer, pt[b * n_pages + SUM_PAGES * g + r], 0, 0, 0)

    return pl.pallas_call(
        functools.partial(_page_sum_kernel, ppb=ppb),
        grid_spec=pltpu.PrefetchScalarGridSpec(
            num_scalar_prefetch=1,
            grid=(bsz, nblk // bps),
            in_specs=[pl.BlockSpec((None, None, page, H_ATT, HD), idx(r)) for r in range(SUM_PAGES)],
            out_specs=pl.BlockSpec((None, bps, H_ATT, HD), lambda b, g, pt: (b, g, 0, 0)),
        ),
        out_shape=jax.ShapeDtypeStruct((bsz, nblk, H_ATT, HD), F32),
        compiler_params=_cparams("parallel", "arbitrary"),
        name="moba_block_sums",
    )(pt_flat, *([cache_k] * SUM_PAGES))


def _moba_select_kernel(q_ref, ks_ref, sel_ref, *, nblk):
    shape = (nblk, H_ATT, HD)
    prod = ks_ref[...] * q_ref[...][None]
    gate = jnp.broadcast_to(jnp.sum(prod, axis=-1, keepdims=True), shape) * (1.0 / MOBA_BLOCK)
    ridx = lax.broadcasted_iota(jnp.int32, shape, 0).astype(F32)
    for s in range(MOBA_TOPK):
        m = jnp.max(gate, axis=0, keepdims=True)
        first = jnp.min(jnp.where(gate == m, ridx, float(nblk)), axis=0, keepdims=True)
        sel_ref[s] = first[0].astype(jnp.int32)
        gate = jnp.where(ridx == first, -jnp.inf, gate)


def _moba_select(q_heads, ksum, bsz):
    nblk = ksum.shape[1]
    return pl.pallas_call(
        functools.partial(_moba_select_kernel, nblk=nblk),
        grid=(bsz,),
        in_specs=[pl.BlockSpec((None, H_ATT, HD), lambda b: (b, 0, 0)),
                  pl.BlockSpec((None, nblk, H_ATT, HD), lambda b: (b, 0, 0, 0))],
        out_specs=pl.BlockSpec((None, MOBA_TOPK, H_ATT, HD), lambda b: (b, 0, 0, 0)),
        out_shape=jax.ShapeDtypeStruct((bsz, MOBA_TOPK, H_ATT, HD), jnp.int32),
        compiler_params=_cparams("parallel"),
        name="moba_select",
    )(q_heads, ksum)


def _moba_sample_kernel(sel_ref, pt_ref, slope_ref, q_ref, kn_ref, vn_ref, *refs,
                        past_len, page, ppb, npg):
    kp, vp = refs[:npg], refs[npg:2 * npg]
    g_ref, o_ref = refs[2 * npg:]
    b = pl.program_id(0)
    h = pl.program_id(1)
    scale = HD ** -0.5
    slope = slope_ref[h]
    shape = (page, H_ATT, HD)
    q = q_ref[...]
    tok = lax.broadcasted_iota(jnp.int32, shape, 0)

    s_own = jnp.broadcast_to(jnp.sum(q * kn_ref[...], axis=-1, keepdims=True), (H_ATT, HD)) * scale
    scores = []
    m = s_own
    for r in range(npg):
        blk_id = sel_ref[(b * H_ATT + h) * MOBA_TOPK + r // ppb]
        dist = (past_len - (blk_id * MOBA_BLOCK + (r % ppb) * page) - tok).astype(F32)
        s = jnp.broadcast_to(jnp.sum(kp[r][...] * q[None], axis=-1, keepdims=True), shape)
        s = s * scale - slope * dist
        scores.append(s)
        m = jnp.maximum(m, jnp.max(s, axis=0))
    e_own = jnp.exp(s_own - m)
    l = e_own
    for r in range(npg):
        scores[r] = jnp.exp(scores[r] - m[None])
        l = l + jnp.sum(scores[r], axis=0)
    inv = 1.0 / l
    acc = (e_own * inv) * vn_ref[...]
    for r in range(npg):
        acc = acc + jnp.sum((scores[r] * inv[None]) * vp[r][...], axis=0)
    o = _rms(acc, g_ref[...])
    live = lax.broadcasted_iota(jnp.int32, (H_ATT, HD), 0) == h
    o_ref[pl.ds(h, 1), :] = jnp.sum(jnp.where(live, o, 0.0), axis=0, keepdims=True)


def _moba_sample(q_heads, kn_heads, vn_heads, cache_k, cache_v, sel_flat, pt_flat, slopes, g_att, layer,
                 bsz, n_pages):
    page = cache_k.shape[2]
    ppb = MOBA_BLOCK // page
    npg = MOBA_TOPK * ppb
    past_len = n_pages * page

    def page_idx(r):
        def idx(b, h, sel, pt):
            blk = sel[(b * H_ATT + h) * MOBA_TOPK + r // ppb]
            return (layer, pt[b * n_pages + ppb * blk + r % ppb], 0, 0, 0)
        return idx

    tok_spec = pl.BlockSpec((None, H_ATT, HD), lambda b, h, sel, pt: (b, 0, 0))
    page_specs = [pl.BlockSpec((None, None, page, H_ATT, HD), page_idx(r)) for r in range(npg)]
    return pl.pallas_call(
        functools.partial(_moba_sample_kernel, past_len=past_len, page=page, ppb=ppb, npg=npg),
        grid_spec=pltpu.PrefetchScalarGridSpec(
            num_scalar_prefetch=2,
            grid=(bsz, H_ATT),
            in_specs=[pl.BlockSpec(memory_space=pltpu.SMEM), tok_spec, tok_spec, tok_spec]
            + page_specs + page_specs
            + [pl.BlockSpec((H_ATT, HD), lambda b, h, sel, pt: (0, 0))],
            out_specs=pl.BlockSpec((None, H_ATT, HD), lambda b, h, sel, pt: (b, 0, 0)),
        ),
        out_shape=jax.ShapeDtypeStruct((bsz, H_ATT, HD), F32),
        compiler_params=_cparams("arbitrary", "arbitrary"),
        name="moba_sample",
    )(sel_flat, pt_flat, slopes, q_heads, kn_heads, vn_heads, *([cache_k] * npg), *([cache_v] * npg),
      g_att.reshape(H_ATT, HD))


def _gla_sample_kernel(q_ref, k_ref, v_ref, r_ref, z_ref, w2_ref, bg_ref, g_ref, s0_ref,
                       o_ref, s_out_ref, *, bsz):
    b = pl.program_id(1)
    a = _log_decay(z_ref[...], w2_ref[...], bg_ref[...])
    live = lax.broadcasted_iota(jnp.int32, (bsz, 1), 0) == b

    def row(x):
        return jnp.sum(jnp.where(live, x, 0.0), axis=0, keepdims=True)

    eye = lax.broadcasted_iota(jnp.int32, (DK, DK), 0) == lax.broadcasted_iota(jnp.int32, (DK, DK), 1)

    def col(x):
        return jnp.sum(jnp.where(eye, jnp.broadcast_to(x, (DK, DK)), 0.0), axis=1, keepdims=True)

    e = jnp.exp(row(a))
    q = row(q_ref[...]) * (DK ** -0.5)
    k = row(k_ref[...])
    v = row(v_ref[...])
    rg = row(r_ref[...])
    s0 = s0_ref[...]
    o = jnp.sum(col(q * e) * s0, axis=0, keepdims=True) + jnp.sum(q * k, axis=1, keepdims=True) * v
    s_out_ref[...] = col(e) * s0 + col(k) * v
    on = _rms(o, g_ref[...]) * (rg * jax.nn.sigmoid(rg))
    o_ref[pl.ds(b, 1), :] = on


def _gla_sample(proj_d, state5, w2p, bg, g_gla, layer, bsz):
    return pl.pallas_call(
        functools.partial(_gla_sample_kernel, bsz=bsz),
        grid=(H_GLA, bsz),
        in_specs=[
            pl.BlockSpec((bsz, DK), lambda h, b: (0, CB_QG + h)),
            pl.BlockSpec((bsz, DK), lambda h, b: (0, CB_KG + h)),
            pl.BlockSpec((bsz, DV), lambda h, b: (0, CB_VG + h)),
            pl.BlockSpec((bsz, DV), lambda h, b: (0, CB_RG + h)),
            pl.BlockSpec((bsz, LANES), lambda h, b: (0, CB_Z)),
            pl.BlockSpec((LANES, DK), lambda h, b: (0, h)),
            pl.BlockSpec((1, DK), lambda h, b: (0, h)),
            pl.BlockSpec((1, DV), lambda h, b: (0, h)),
            pl.BlockSpec((None, None, None, DK, DV), lambda h, b: (layer, b, h, 0, 0)),
        ],
        out_specs=[
            pl.BlockSpec((bsz, DV), lambda h, b: (0, h)),
            pl.BlockSpec((None, None, DK, DV), lambda h, b: (b, h, 0, 0)),
        ],
        out_shape=[jax.ShapeDtypeStruct((bsz, GLA_VW), F32),
                   jax.ShapeDtypeStruct((bsz, H_GLA, DK, DV), F32)],
        compiler_params=_cparams("arbitrary", "arbitrary"),
        name="gla_sample",
    )(proj_d, proj_d, proj_d, proj_d, proj_d, w2p, bg.reshape(1, GLA_KW), g_gla.reshape(1, GLA_VW),
      state5)


def _out_proj_kernel(oa_ref, og_ref, wa_ref, wg_ref, x_ref, g_ref, h_ref, xnt_ref):
    h = x_ref[...] + _mm(oa_ref[...], wa_ref[...]) + _mm(og_ref[...], wg_ref[...])
    h_ref[...] = h
    xnt_ref[...] = _rms(h, g_ref[...]).T.astype(xnt_ref.dtype)


def _out_proj(oa, og, wa, wg, x, g2, tm):
    t, d = x.shape
    return pl.pallas_call(
        _out_proj_kernel,
        grid=(t // tm,),
        in_specs=[pl.BlockSpec((tm, ATT_W), lambda i: (i, 0)),
                  pl.BlockSpec((tm, GLA_VW), lambda i: (i, 0)),
                  pl.BlockSpec((ATT_W, d), lambda i: (0, 0)),
                  pl.BlockSpec((GLA_VW, d), lambda i: (0, 0)),
                  pl.BlockSpec((tm, d), lambda i: (i, 0)),
                  pl.BlockSpec((1, d), lambda i: (0, 0))],
        out_specs=[pl.BlockSpec((tm, d), lambda i: (i, 0)),
                   pl.BlockSpec((d, tm), lambda i: (0, i))],
        out_shape=[jax.ShapeDtypeStruct((t, d), F32), jax.ShapeDtypeStruct((d, t), wa.dtype)],
        compiler_params=_cparams("parallel"),
        name="out_proj",
    )(oa, og, wa, wg, x, g2.reshape(1, d))


def _top_values(s, n):
    vals = []
    for _ in range(n):
        m = jnp.max(s, axis=0, keepdims=True)
        vals.append(m)
        s = jnp.where(s == m, -jnp.inf, s)
    return vals


def _peer_route_kernel(xnt_ref, wq_ref, sk_ref, thr_ref, e1_ref, s2_ref, e2_ref, q_scr):
    q_scr[...] = _mm(wq_ref[...], xnt_ref[...])
    half = PEER_QDIM // 2
    k = PEER_TOPK
    tm = q_scr.shape[1]
    rid = lax.broadcasted_iota(jnp.int32, (SUBLANES, tm), 0)

    def head(h, _):
        r1 = pl.multiple_of(h * PEER_QDIM, PEER_QDIM)
        r2 = pl.multiple_of(h * PEER_QDIM + half, half)
        s1 = jnp.dot(sk_ref[h, 0], q_scr[pl.ds(r1, half), :], precision=HI, preferred_element_type=F32)
        s2 = jnp.dot(sk_ref[h, 1], q_scr[pl.ds(r2, half), :], precision=HI, preferred_element_type=F32)
        v1 = _top_values(s1, k)
        v2 = _top_values(s2, k)
        v1s = jnp.concatenate(v1, axis=0)
        v2s = jnp.concatenate(v2, axis=0)
        parts = [v1[0] + v2s]
        for a in range(1, SUBLANES):
            nb = k // (a + 1)
            parts.append(jnp.where(rid < nb, v1[a] + v2s[:SUBLANES], -jnp.inf))
        parts.append(v1s[SUBLANES:] + v2[0])
        cand = jnp.concatenate(parts, axis=0)
        tau = _top_values(cand, k)[-1]
        smax = v1[0] + v2[0]
        z = jnp.sum(jnp.where(cand >= tau, jnp.exp(cand - smax), 0.0), axis=0, keepdims=True)
        thr = jnp.full(s1.shape, jnp.inf, F32)
        for a in range(k):
            ta = jnp.min(jnp.where(v1[a] + v2s >= tau, v2s, jnp.inf), axis=0, keepdims=True)
            thr = jnp.where(s1 == v1[a], ta, thr)
        rows = pl.ds(pl.multiple_of(h * N_KEYS, N_KEYS), N_KEYS)
        thr_ref[rows, :] = thr
        e1_ref[rows, :] = jnp.exp(s1 - v1[0]) / z
        s2_ref[h] = s2
        e2_ref[h] = jnp.exp(s2 - v2[0])
        return 0

    lax.fori_loop(0, PEER_HEADS, head, 0)


def _peer_route(xnt, wqt, sub_keys, tm):
    d, t = xnt.shape
    nq = wqt.shape[0]
    flat = jax.ShapeDtypeStruct((PEER_HEADS * N_KEYS, t), F32)
    big = jax.ShapeDtypeStruct((PEER_HEADS, N_KEYS, t), F32)
    fspec = pl.BlockSpec((PEER_HEADS * N_KEYS, tm), lambda i: (0, i))
    bspec = pl.BlockSpec((PEER_HEADS, N_KEYS, tm), lambda i: (0, 0, i))
    return pl.pallas_call(
        _peer_route_kernel,
        grid=(t // tm,),
        in_specs=[pl.BlockSpec((d, tm), lambda i: (0, i)),
                  pl.BlockSpec((nq, d), lambda i: (0, 0)),
                  pl.BlockSpec((PEER_HEADS, 2, N_KEYS, PEER_QDIM // 2), lambda i: (0, 0, 0, 0))],
        out_specs=[fspec, fspec, bspec, bspec],
        out_shape=[flat, flat, big, big],
        scratch_shapes=[pltpu.VMEM((nq, tm), F32)],
        compiler_params=_cparams("parallel"),
        name="peer_route",
    )(xnt, wqt, sub_keys)


def _peer_dense_kernel(xnt_ref, u_ref, vt_ref, thr_ref, e1_ref, s2_ref, e2_ref,
                       yt_ref, p_scr, ht_scr, *, te, tm):
    j = pl.program_id(1)

    @pl.when(j == 0)
    def _():
        yt_ref[...] = jnp.zeros_like(yt_ref)

    ht_scr[...] = _mm(u_ref[...], xnt_ref[...])
    n1 = te // N_KEYS
    assert n1 % SUBLANES == 0
    row0 = pl.multiple_of(j * n1, SUBLANES)
    for il in range(n1):
        rs = slice(il * N_KEYS, (il + 1) * N_KEYS)
        for c in range(tm // LANES):
            cs = slice(c * LANES, (c + 1) * LANES)
            w = jnp.zeros((N_KEYS, LANES), F32)
            for h in range(PEER_HEADS):
                row = pl.ds(row0 + (h * N_KEYS + il), 1)
                w = w + jnp.where(s2_ref[h, :, cs] >= thr_ref[row, :][:, cs],
                                  e1_ref[row, :][:, cs] * e2_ref[h, :, cs], 0.0)
            ht = ht_scr[rs, cs]
            act = 0.5 * ht * (1.0 + lax.erf(ht * (2.0 ** -0.5)))
            p_scr[rs, cs] = (w * act).astype(p_scr.dtype)
    yt_ref[...] += _mm(vt_ref[...], p_scr[...])


def _peer_dense(xnt, u_b, vt_b, thr, e1, s2, e2, tm, te):
    d, t = xnt.shape
    ne = u_b.shape[0]
    fspec = pl.BlockSpec((PEER_HEADS * N_KEYS, tm), lambda i, j: (0, i))
    bspec = pl.BlockSpec((PEER_HEADS, N_KEYS, tm), lambda i, j: (0, 0, i))
    return pl.pallas_call(
        functools.partial(_peer_dense_kernel, te=te, tm=tm),
        grid=(t // tm, ne // te),
        in_specs=[pl.BlockSpec((d, tm), lambda i, j: (0, i)),
                  pl.BlockSpec((te, d), lambda i, j: (j, 0)),
                  pl.BlockSpec((d, te), lambda i, j: (0, j)),
                  fspec, fspec, bspec, bspec],
        out_specs=pl.BlockSpec((d, tm), lambda i, j: (0, i)),
        out_shape=jax.ShapeDtypeStruct((d, t), F32),
        scratch_shapes=[pltpu.VMEM((te, tm), u_b.dtype), pltpu.VMEM((te, tm), F32)],
        compiler_params=_cparams("parallel", "arbitrary"),
        name="peer_dense",
    )(xnt, u_b, vt_b, thr, e1, s2, e2)


def _token_tiles(t):
    if t >= 512:
        return 512, 256, 256, 512
    return t, t, t, t


def _layer_weights(l, norm1_g, w_in, w_gate2, b_gate, att_norm_g, gla_norm_g, w_out, norm2_g, w_pq,
                   peer_sub_keys, peer_u, peer_v):
    f32 = dict(
        g1=norm1_g[l],
        w_in=jnp.pad(w_in[l], ((0, 0), (0, N_IN_PAD - N_IN))),
        w2p=jnp.pad(w_gate2[l], ((0, LANES - GATE_RANK), (0, 0))),
        bg=b_gate[l], g_att=att_norm_g[l], g_gla=gla_norm_g[l],
        wo_a=w_out[l, :ATT_W], wo_g=w_out[l, ATT_W:],
        g2=norm2_g[l],
        wqt=w_pq[l].T,
        sub_keys=peer_sub_keys[l],
        u=peer_u[l],
        vt=peer_v[l].T,
    )
    bf16 = dict(f32)
    for name in ("w_in", "wo_a", "wo_g", "wqt", "u", "vt"):
        bf16[name] = f32[name].astype(BF16)
    return f32, bf16


def _peer(h, xnt, w, tm_route, tm_exp):
    route = _peer_route(xnt, w["wqt"], w["sub_keys"], tm_route)
    return _peer_dense(xnt, w["u"], w["vt"], *route, tm_exp, EXPERT_TILE)


def _prompt_layer(x, yt, w, slopes, bsz, seq):
    tm_in, tm_out, tm_route, tm_exp = _token_tiles(bsz * seq)
    x, proj = _norm_matmul(x, yt, w["g1"], w["w_in"], tm_in, 896)
    oa = _moba_prompt(proj, slopes, w["g_att"], bsz, seq)
    og, state = _gla_prompt(proj, w["w2p"], w["bg"], w["g_gla"], bsz, seq)
    h, xnt = _out_proj(oa, og, w["wo_a"], w["wo_g"], x, w["g2"], tm_out)
    return h, _peer(h, xnt, w, tm_route, tm_exp), proj, state


def _sample_layer(x, yt, w, slopes, cache_k, cache_v, state_gla, pt_flat, layer, bsz, n_pages):
    tm_in, tm_out, tm_route, tm_exp = _token_tiles(SAMPLE_PAD)
    x, proj = _norm_matmul(x, yt, w["g1"], w["w_in"], tm_in, 896)
    q_heads = proj[:bsz, :ATT_W].reshape(bsz, H_ATT, HD)
    kn_heads = proj[:bsz, ATT_W:2 * ATT_W].reshape(bsz, H_ATT, HD)
    vn_heads = proj[:bsz, 2 * ATT_W:3 * ATT_W].reshape(bsz, H_ATT, HD)
    ksum = _moba_block_sums(cache_k, pt_flat, layer, bsz, n_pages)
    sel = _moba_select(q_heads, ksum, bsz)
    sel_flat = sel[:, :, :, 0].transpose(0, 2, 1).reshape(-1)
    oa = _moba_sample(q_heads, kn_heads, vn_heads, cache_k, cache_v, sel_flat, pt_flat, slopes, w["g_att"],
                      layer, bsz, n_pages).reshape(bsz, ATT_W)
    og, state = _gla_sample(proj, state_gla, w["w2p"], w["bg"], w["g_gla"], layer, bsz)
    oa = jnp.pad(oa, ((0, SAMPLE_PAD - bsz), (0, 0))).astype(w["wo_a"].dtype)
    og = jnp.pad(og, ((0, SAMPLE_PAD - bsz), (0, 0))).astype(w["wo_a"].dtype)
    h, xnt = _out_proj(oa, og, w["wo_a"], w["wo_g"], x, w["g2"], tm_out)
    return h, _peer(h, xnt, w, tm_route, tm_exp), proj, state


def kernel(x_prompt, x_sample, cache_k, cache_v, state_gla, page_table, norm1_g, w_in, w_gate2, b_gate,
           att_norm_g, gla_norm_g, w_out, norm2_g, w_pq, peer_sub_keys, peer_u, peer_v, final_norm_g):
    bp, seq, d = x_prompt.shape
    bd, sd, _ = x_sample.shape
    depth = w_in.shape[0]
    n_pages = page_table.shape[1]
    assert sd == 1 and d == D_MODEL and seq % MOBA_BLOCK == 0 and bd <= SUBLANES
    tp = bp * seq

    slopes = 2.0 ** (-(8.0 / H_ATT) * jnp.arange(1, H_ATT + 1, dtype=F32))
    pt_flat = page_table.reshape(-1).astype(jnp.int32)

    xp = x_prompt.reshape(tp, d)
    xd = jnp.pad(x_sample.reshape(bd, d), ((0, SAMPLE_PAD - bd), (0, 0)))
    ytp = ytd = None
    kp_l, vp_l, sp_l, kd_l, vd_l, sd_l = [], [], [], [], [], []

    for l in range(depth):
        w_f32, w_bf16 = _layer_weights(l, norm1_g, w_in, w_gate2, b_gate, att_norm_g, gla_norm_g, w_out,
                                       norm2_g, w_pq, peer_sub_keys, peer_u, peer_v)
        xp, ytp, proj_p, s_p = _prompt_layer(xp, ytp, w_bf16, slopes, bp, seq)
        xd, ytd, proj_d, s_d = _sample_layer(xd, ytd, w_f32, slopes, cache_k, cache_v, state_gla, pt_flat,
                                             l, bd, n_pages)
        kp_l.append(proj_p[:, ATT_W:2 * ATT_W].reshape(bp, seq, H_ATT, HD))
        vp_l.append(proj_p[:, 2 * ATT_W:3 * ATT_W].reshape(bp, seq, H_ATT, HD))
        sp_l.append(s_p)
        kd_l.append(proj_d[:bd, ATT_W:2 * ATT_W].reshape(bd, 1, H_ATT, HD))
        vd_l.append(proj_d[:bd, 2 * ATT_W:3 * ATT_W].reshape(bd, 1, H_ATT, HD))
        sd_l.append(s_d)

    y_prompt = _final_norm(xp, ytp, final_norm_g, 512).reshape(bp, seq, d)
    y_sample = _final_norm(xd, ytd, final_norm_g, SAMPLE_PAD)[:bd].reshape(bd, 1, d)
    return (y_prompt, y_sample, jnp.stack(kp_l), jnp.stack(vp_l), jnp.stack(sp_l),
            jnp.stack(kd_l), jnp.stack(vd_l), jnp.stack(sd_l))
```

```python
import functools

import jax
import jax.numpy as jnp
from jax import lax
from jax.experimental import pallas as pl
from jax.experimental.pallas import tpu as pltpu

F32 = jnp.float32
BF16 = jnp.bfloat16
HI = lax.Precision.HIGHEST

LANES = 128
SUBLANES = 8
VMEM_LIMIT_BYTES = 56 * 1024 * 1024

D_MODEL = 2048
H_ATT = 8
HD = 128
MOBA_BLOCK = 256
MOBA_TOPK = 3
H_GLA = 4
DK = 128
DV = 256
GATE_RANK = 16
GATE_TAU = 16.0
GLA_CHUNK = 64
ATT_W = H_ATT * HD
GLA_KW = H_GLA * DK
GLA_VW = H_GLA * DV
N_IN = 3 * ATT_W + 2 * GLA_KW + 2 * GLA_VW + GATE_RANK
N_IN_PAD = N_IN - GATE_RANK + LANES
PEER_HEADS = 8
N_KEYS = 128
N_EXPERTS = N_KEYS * N_KEYS
PEER_TOPK = 16
PEER_QDIM = 256
EPS = 1e-6

CB_QA = 0
CB_KA = ATT_W // LANES
CB_VA = 2 * ATT_W // LANES
CB_QG = 3 * ATT_W // LANES
CB_KG = CB_QG + GLA_KW // LANES
CB_VG = (3 * ATT_W + 2 * GLA_KW) // DV
CB_RG = CB_VG + GLA_VW // DV
CB_Z = (N_IN - GATE_RANK) // LANES

SAMPLE_PAD = 128
EXPERT_TILE = SUBLANES * N_KEYS


def _cparams(*sem):
    return pltpu.CompilerParams(dimension_semantics=sem, vmem_limit_bytes=VMEM_LIMIT_BYTES)


def _rms(x, g):
    return x * lax.rsqrt(jnp.mean(x * x, axis=-1, keepdims=True) + EPS) * g


def _split_bf16(x):
    hi = x.astype(BF16)
    return hi, (x - hi.astype(F32)).astype(BF16)


def _mm(a, b):
    if a.dtype != F32:
        return jnp.dot(a, b, preferred_element_type=F32)
    a_hi, a_lo = _split_bf16(a)
    b_hi, b_lo = _split_bf16(b)
    return (jnp.dot(a_lo, b_hi, preferred_element_type=F32)
            + jnp.dot(a_hi, b_lo, preferred_element_type=F32)
            + jnp.dot(a_hi, b_hi, preferred_element_type=F32))


def _dot_nt(a, b, precision=None):
    return lax.dot_general(a, b, (((1,), (1,)), ((), ())), precision=precision,
                           preferred_element_type=F32)


def _mm_nt(a, b):
    a_hi, a_lo = _split_bf16(a)
    b_hi, b_lo = _split_bf16(b)
    return _dot_nt(a_lo, b_hi) + _dot_nt(a_hi, b_lo) + _dot_nt(a_hi, b_hi)


def _norm_matmul_kernel(*refs, has_y):
    if has_y:
        h_ref, yt_ref, g_ref, w_ref, x_out_ref, o_ref, xn_scr = refs
    else:
        x_ref, g_ref, w_ref, o_ref, xn_scr = refs

    @pl.when(pl.program_id(1) == 0)
    def _():
        if has_y:
            x = h_ref[...] + yt_ref[...].T
            x_out_ref[...] = x
        else:
            x = x_ref[...]
        xn_scr[...] = _rms(x, g_ref[...]).astype(xn_scr.dtype)

    o_ref[...] = _mm(xn_scr[...], w_ref[...])


def _norm_matmul(x, yt, g, w, tm, tn):
    t, d = x.shape
    n = w.shape[1]
    has_y = yt is not None
    in_specs = [pl.BlockSpec((tm, d), lambda i, j: (i, 0))]
    args = [x]
    if has_y:
        in_specs.append(pl.BlockSpec((d, tm), lambda i, j: (0, i)))
        args.append(yt)
    in_specs += [pl.BlockSpec((1, d), lambda i, j: (0, 0)),
                 pl.BlockSpec((d, tn), lambda i, j: (0, j))]
    args += [g.reshape(1, d), w]
    out_shape = [jax.ShapeDtypeStruct((t, n), F32)]
    out_specs = [pl.BlockSpec((tm, tn), lambda i, j: (i, j))]
    if has_y:
        out_shape.insert(0, jax.ShapeDtypeStruct((t, d), F32))
        out_specs.insert(0, pl.BlockSpec((tm, d), lambda i, j: (i, 0)))
    res = pl.pallas_call(
        functools.partial(_norm_matmul_kernel, has_y=has_y),
        grid=(t // tm, n // tn),
        in_specs=in_specs, out_specs=out_specs, out_shape=out_shape,
        scratch_shapes=[pltpu.VMEM((tm, d), w.dtype)],
        compiler_params=_cparams("parallel", "arbitrary"),
        name="norm_in_proj",
    )(*args)
    if has_y:
        return res[0], res[1]
    return x, res[0]


def _final_norm_kernel(h_ref, yt_ref, g_ref, o_ref):
    o_ref[...] = _rms(h_ref[...] + yt_ref[...].T, g_ref[...])


def _final_norm(h, yt, g, tm):
    t, d = h.shape
    return pl.pallas_call(
        _final_norm_kernel,
        grid=(t // tm,),
        in_specs=[pl.BlockSpec((tm, d), lambda i: (i, 0)),
                  pl.BlockSpec((d, tm), lambda i: (0, i)),
                  pl.BlockSpec((1, d), lambda i: (0, 0))],
        out_specs=pl.BlockSpec((tm, d), lambda i: (i, 0)),
        out_shape=jax.ShapeDtypeStruct((t, d), F32),
        compiler_params=_cparams("parallel"),
        name="final_norm",
    )(h, yt, g.reshape(1, d))


def _moba_prompt_kernel(slope_ref, q_ref, k_ref, v_ref, g_ref, o_ref,
                        kb_scr, vb_scr, kmean_scr, *, nb, group):
    h = pl.program_id(1)
    i = pl.program_id(2)
    blk = MOBA_BLOCK

    @pl.when(i == 0)
    def _():
        k = k_ref[...]
        kb_scr[...] = k.astype(BF16)
        vb_scr[...] = v_ref[...].astype(BF16)
        kmean_scr[...] = jnp.mean(k.reshape(nb, blk, HD), axis=1)

    slope = slope_ref[h]
    scale = HD ** -0.5
    q = q_ref[...]
    qb = q.astype(BF16)

    gate = _dot_nt(q, kmean_scr[...], HI)
    nidx = lax.broadcasted_iota(jnp.int32, (blk, nb), 1)
    nidx_f = nidx.astype(F32)
    gate = jnp.where(nidx < i, gate, -jnp.inf)
    sel = jnp.zeros((blk, nb), F32)
    for s in range(MOBA_TOPK):
        m = jnp.max(gate, axis=1, keepdims=True)
        first = jnp.min(jnp.where(gate == m, nidx_f, float(nb)), axis=1, keepdims=True)
        pick = jnp.logical_and(nidx_f == first, jnp.full((blk, nb), i, jnp.int32) > s)
        sel = jnp.where(pick, 1.0, sel)
        gate = jnp.where(pick, -jnp.inf, gate)

    rr = lax.broadcasted_iota(jnp.int32, (blk, blk), 0)
    cc = lax.broadcasted_iota(jnp.int32, (blk, blk), 1)
    rel_i = rr - cc
    rel = rel_i.astype(F32)

    row0 = pl.multiple_of(i * blk, blk)
    s_own = _dot_nt(qb, kb_scr[pl.ds(row0, blk), :]) * scale - slope * rel
    s_own = jnp.where(rel >= 0, s_own, -jnp.inf)
    m0 = jnp.max(s_own, axis=1, keepdims=True)
    p0 = jnp.exp(s_own - m0)
    l0 = jnp.sum(p0, axis=1, keepdims=True)
    acc0 = jnp.dot(p0.astype(BF16), vb_scr[pl.ds(row0, blk), :], preferred_element_type=F32)

    def body(g, carry):
        m, l, acc = carry
        s_parts = []
        for c in range(group):
            j = g * group + c
            rj = pl.multiple_of(j * blk, blk)
            use = jnp.sum(jnp.where(nidx == j, sel, 0.0), axis=1, keepdims=True)
            dist = (rel_i + (i - j) * blk).astype(F32)
            s = _dot_nt(qb, kb_scr[pl.ds(rj, blk), :]) * scale - slope * dist
            s_parts.append(jnp.where(use > 0.0, s, -jnp.inf))
        m_new = m
        for s in s_parts:
            m_new = jnp.maximum(m_new, jnp.max(s, axis=1, keepdims=True))
        alpha = jnp.exp(m - m_new)
        l = alpha * l
        acc = alpha * acc
        for c, s in enumerate(s_parts):
            rj = pl.multiple_of((g * group + c) * blk, blk)
            p = jnp.exp(s - m_new)
            l = l + jnp.sum(p, axis=1, keepdims=True)
            acc = acc + jnp.dot(p.astype(BF16), vb_scr[pl.ds(rj, blk), :], preferred_element_type=F32)
        return m_new, l, acc

    _, l, acc = lax.fori_loop(0, (i + group - 1) // group, body, (m0, l0, acc0))
    o = acc / l
    o_ref[...] = _rms(o, g_ref[...]).astype(o_ref.dtype)


def _moba_prompt(proj, slopes, g_att, bsz, seq):
    nb = seq // MOBA_BLOCK
    group = next(g for g in (4, 2, 1) if nb % g == 0)
    return pl.pallas_call(
        functools.partial(_moba_prompt_kernel, nb=nb, group=group),
        grid_spec=pltpu.PrefetchScalarGridSpec(
            num_scalar_prefetch=0,
            grid=(bsz, H_ATT, nb),
            in_specs=[
                pl.BlockSpec(memory_space=pltpu.SMEM),
                pl.BlockSpec((MOBA_BLOCK, HD), lambda b, h, i: (b * nb + i, CB_QA + h)),
                pl.BlockSpec((seq, HD), lambda b, h, i: (b, CB_KA + h)),
                pl.BlockSpec((seq, HD), lambda b, h, i: (b, CB_VA + h)),
                pl.BlockSpec((1, HD), lambda b, h, i: (0, h)),
            ],
            out_specs=pl.BlockSpec((MOBA_BLOCK, HD), lambda b, h, i: (b * nb + i, h)),
            scratch_shapes=[pltpu.VMEM((seq, HD), BF16), pltpu.VMEM((seq, HD), BF16),
                            pltpu.VMEM((nb, HD), F32)],
        ),
        out_shape=jax.ShapeDtypeStruct((bsz * seq, ATT_W), BF16),
        compiler_params=_cparams("parallel", "parallel", "arbitrary"),
        name="moba_prompt",
    )(slopes, proj, proj, proj, g_att.reshape(1, ATT_W))


def _log_decay(z, w2, bg):
    x = _mm(z, w2) + bg
    return (jnp.minimum(x, 0.0) - jnp.log1p(jnp.exp(-jnp.abs(x)))) * (1.0 / GATE_TAU)


GLA_SUB = 16
GLA_HEADS_PER_STEP = 4
GLA_SEQ_TILE = 1024


def _gla_prompt_kernel(q_ref, k_ref, v_ref, r_ref, z_ref, w2_ref, bg_ref, g_ref,
                       o_ref, s_out_ref, s_scr, b_scr, q_scr, oi_scr, k_scr, v_scr, *, nchunk, heads):
    c, sub = GLA_CHUNK, GLA_SUB
    t = pl.program_id(2)

    @pl.when(t == 0)
    def _():
        s_scr[...] = jnp.zeros_like(s_scr)

    ri = lax.broadcasted_iota(jnp.int32, (c, c), 0)
    ci = lax.broadcasted_iota(jnp.int32, (c, c), 1)
    tril = (ri >= ci).astype(F32)
    tril_prev = (ci < (ri // sub) * sub).astype(F32)
    eye = lax.broadcasted_iota(jnp.int32, (DK, DK), 0) == lax.broadcasted_iota(jnp.int32, (DK, DK), 1)

    def chunk(n, _):
        r0 = pl.multiple_of(n * c, c)
        rows = pl.ds(r0, c)
        z = z_ref[rows, :]
        q_all = q_ref[rows, :] * (DK ** -0.5)
        k_all = k_ref[rows, :]
        v_all = v_ref[rows, :]
        r_all = r_ref[rows, :]
        k_scr[...] = k_all
        v_scr[...] = v_all
        outs = []
        for hh in range(heads):
            ks = slice(hh * DK, (hh + 1) * DK)
            vs = slice(hh * DV, (hh + 1) * DV)
            a = _log_decay(z, w2_ref[:, ks], bg_ref[:, ks])
            b = _mm(tril, a)
            bprev = _mm(tril_prev, a)
            q = q_all[:, ks]
            k = k_all[:, ks]
            v = v_all[:, vs]
            s_old = s_scr[hh]
            oi_scr[hh] = _mm(q * jnp.exp(b), s_old)
            b_scr[hh] = b
            q_scr[hh] = q
            qt = q * jnp.exp(b - bprev)
            for i0 in range(sub, c, sub):
                ref = b_scr[hh, i0 - 1:i0, :]
                kh = k[:i0] * jnp.exp(ref - b[:i0])
                a_blk = _mm_nt(qt[i0:i0 + sub], kh)
                oi_scr[hh, i0:i0 + sub, :] = oi_scr[hh, i0:i0 + sub, :] + _mm(a_blk, v[:i0])
            rid = lax.broadcasted_iota(jnp.int32, (sub, DK), 0)
            for i0 in range(0, c, sub):
                bi = b_scr[hh, i0:i0 + sub, :]
                qi = q_scr[hh, i0:i0 + sub, :]
                acc = jnp.zeros((sub, DV), F32)
                for jj in range(sub):
                    j = i0 + jj
                    bj = b_scr[hh, j:j + 1, :]
                    kj = k_scr[j:j + 1, ks]
                    vj = v_scr[j:j + 1, vs]
                    w = jnp.exp(jnp.where(rid >= jj, bi - bj, -jnp.inf)) * (qi * kj)
                    acc = acc + jnp.sum(w, axis=1, keepdims=True) * vj
                oi_scr[hh, i0:i0 + sub, :] = oi_scr[hh, i0:i0 + sub, :] + acc
            b_last = b[c - 1:c, :]
            e_last = jnp.exp(b_last)
            e_col = jnp.sum(jnp.where(eye, jnp.broadcast_to(e_last, (DK, DK)), 0.0), axis=1, keepdims=True)
            kd = k * jnp.exp(b_last - b)
            s_scr[hh] = e_col * s_old + _mm(kd.T, v)
            rg = r_all[:, vs]
            outs.append(_rms(oi_scr[hh], g_ref[:, vs]) * (rg * jax.nn.sigmoid(rg)))
        o_ref[rows, :] = jnp.concatenate(outs, axis=1).astype(o_ref.dtype)
        return 0

    lax.fori_loop(0, nchunk, chunk, 0)

    @pl.when(t == pl.num_programs(2) - 1)
    def _():
        s_out_ref[...] = s_scr[...]


def _gla_prompt(proj, w2p, bg, g_gla, bsz, seq):
    hg = GLA_HEADS_PER_STEP
    ts = min(GLA_SEQ_TILE, seq)
    nt = seq // ts
    assert seq % ts == 0 and ts % GLA_CHUNK == 0 and H_GLA % hg == 0
    kw, vw = hg * DK, hg * DV
    cq, ck = CB_QG * LANES // kw, CB_KG * LANES // kw
    cv, cr = CB_VG * DV // vw, CB_RG * DV // vw
    return pl.pallas_call(
        functools.partial(_gla_prompt_kernel, nchunk=ts // GLA_CHUNK, heads=hg),
        grid=(bsz, H_GLA // hg, nt),
        in_specs=[
            pl.BlockSpec((ts, kw), lambda b, g, t: (b * nt + t, cq + g)),
            pl.BlockSpec((ts, kw), lambda b, g, t: (b * nt + t, ck + g)),
            pl.BlockSpec((ts, vw), lambda b, g, t: (b * nt + t, cv + g)),
            pl.BlockSpec((ts, vw), lambda b, g, t: (b * nt + t, cr + g)),
            pl.BlockSpec((ts, LANES), lambda b, g, t: (b * nt + t, CB_Z)),
            pl.BlockSpec((LANES, kw), lambda b, g, t: (0, g)),
            pl.BlockSpec((1, kw), lambda b, g, t: (0, g)),
            pl.BlockSpec((1, vw), lambda b, g, t: (0, g)),
        ],
        out_specs=[
            pl.BlockSpec((ts, vw), lambda b, g, t: (b * nt + t, g)),
            pl.BlockSpec((None, hg, DK, DV), lambda b, g, t: (b, g, 0, 0)),
        ],
        out_shape=[jax.ShapeDtypeStruct((bsz * seq, GLA_VW), BF16),
                   jax.ShapeDtypeStruct((bsz, H_GLA, DK, DV), F32)],
        scratch_shapes=[pltpu.VMEM((hg, DK, DV), F32), pltpu.VMEM((hg, GLA_CHUNK, DK), F32),
                        pltpu.VMEM((hg, GLA_CHUNK, DK), F32), pltpu.VMEM((hg, GLA_CHUNK, DV), F32),
                        pltpu.VMEM((GLA_CHUNK, kw), F32), pltpu.VMEM((GLA_CHUNK, vw), F32)],
        compiler_params=_cparams("parallel", "parallel", "arbitrary"),
        name="gla_prompt",
    )(proj, proj, proj, proj, proj, w2p, bg.reshape(1, GLA_KW), g_gla.reshape(1, GLA_VW))


SUM_PAGES = 8


def _page_sum_kernel(pt_ref, *refs, ppb):
    pages, o_ref = refs[:-1], refs[-1]
    for n in range(len(pages) // ppb):
        tot = jnp.sum(pages[n * ppb][...], axis=0)
        for r in range(1, ppb):
            tot = tot + jnp.sum(pages[n * ppb + r][...], axis=0)
        o_ref[n] = tot


def _moba_block_sums(cache_k, pt_flat, layer, bsz, n_pages):
    page = cache_k.shape[2]
    ppb = MOBA_BLOCK // page
    nblk = n_pages // ppb
    bps = SUM_PAGES // ppb
    assert nblk % bps == 0

    def idx(r):
        return lambda b, g, pt: (layer, pt[b * n_pages + SUM_PAGES * g + r], 0, 0, 0)

    return pl.pallas_call(
        functools.partial(_page_sum_kernel, ppb=ppb),
        grid_spec=pltpu.PrefetchScalarGridSpec(
            num_scalar_prefetch=1,
            grid=(bsz, nblk // bps),
            in_specs=[pl.BlockSpec((None, None, page, H_ATT, HD), idx(r)) for r in range(SUM_PAGES)],
            out_specs=pl.BlockSpec((None, bps, H_ATT, HD), lambda b, g, pt: (b, g, 0, 0)),
        ),
        out_shape=jax.ShapeDtypeStruct((bsz, nblk, H_ATT, HD), F32),
        compiler_params=_cparams("parallel", "arbitrary"),
        name="moba_block_sums",
    )(pt_flat, *([cache_k] * SUM_PAGES))


def _moba_select_kernel(q_ref, ks_ref, sel_ref, *, nblk):
    shape = (nblk, H_ATT, HD)
    prod = ks_ref[...] * q_ref[...][None]
    gate = jnp.broadcast_to(jnp.sum(prod, axis=-1, keepdims=True), shape) * (1.0 / MOBA_BLOCK)
    ridx = lax.broadcasted_iota(jnp.int32, shape, 0).astype(F32)
    for s in range(MOBA_TOPK):
        m = jnp.max(gate, axis=0, keepdims=True)
        first = jnp.min(jnp.where(gate == m, ridx, float(nblk)), axis=0, keepdims=True)
        sel_ref[s] = first[0].astype(jnp.int32)
        gate = jnp.where(ridx == first, -jnp.inf, gate)


def _moba_select(q_heads, ksum, bsz):
    nblk = ksum.shape[1]
    return pl.pallas_call(
        functools.partial(_moba_select_kernel, nblk=nblk),
        grid=(bsz,),
        in_specs=[pl.BlockSpec((None, H_ATT, HD), lambda b: (b, 0, 0)),
                  pl.BlockSpec((None, nblk, H_ATT, HD), lambda b: (b, 0, 0, 0))],
        out_specs=pl.BlockSpec((None, MOBA_TOPK, H_ATT, HD), lambda b: (b, 0, 0, 0)),
        out_shape=jax.ShapeDtypeStruct((bsz, MOBA_TOPK, H_ATT, HD), jnp.int32),
        compiler_params=_cparams("parallel"),
        name="moba_select",
    )(q_heads, ksum)


def _moba_sample_kernel(sel_ref, pt_ref, slope_ref, q_ref, kn_ref, vn_ref, *refs,
                        past_len, page, ppb, npg):
    kp, vp = refs[:npg], refs[npg:2 * npg]
    g_ref, o_ref = refs[2 * npg:]
    b = pl.program_id(0)
    h = pl.program_id(1)
    scale = HD ** -0.5
    slope = slope_ref[h]
    shape = (page, H_ATT, HD)
    q = q_ref[...]
    tok = lax.broadcasted_iota(jnp.int32, shape, 0)

    s_own = jnp.broadcast_to(jnp.sum(q * kn_ref[...], axis=-1, keepdims=True), (H_ATT, HD)) * scale
    scores = []
    m = s_own
    for r in range(npg):
        blk_id = sel_ref[(b * H_ATT + h) * MOBA_TOPK + r // ppb]
        dist = (past_len - (blk_id * MOBA_BLOCK + (r % ppb) * page) - tok).astype(F32)
        s = jnp.broadcast_to(jnp.sum(kp[r][...] * q[None], axis=-1, keepdims=True), shape)
        s = s * scale - slope * dist
        scores.append(s)
        m = jnp.maximum(m, jnp.max(s, axis=0))
    e_own = jnp.exp(s_own - m)
    l = e_own
    for r in range(npg):
        scores[r] = jnp.exp(scores[r] - m[None])
        l = l + jnp.sum(scores[r], axis=0)
    inv = 1.0 / l
    acc = (e_own * inv) * vn_ref[...]
    for r in range(npg):
        acc = acc + jnp.sum((scores[r] * inv[None]) * vp[r][...], axis=0)
    o = _rms(acc, g_ref[...])
    live = lax.broadcasted_iota(jnp.int32, (H_ATT, HD), 0) == h
    o_ref[pl.ds(h, 1), :] = jnp.sum(jnp.where(live, o, 0.0), axis=0, keepdims=True)


def _moba_sample(q_heads, kn_heads, vn_heads, cache_k, cache_v, sel_flat, pt_flat, slopes, g_att, layer,
                 bsz, n_pages):
    page = cache_k.shape[2]
    ppb = MOBA_BLOCK // page
    npg = MOBA_TOPK * ppb
    past_len = n_pages * page

    def page_idx(r):
        def idx(b, h, sel, pt):
            blk = sel[(b * H_ATT + h) * MOBA_TOPK + r // ppb]
            return (layer, pt[b * n_pages + ppb * blk + r % ppb], 0, 0, 0)
        return idx

    tok_spec = pl.BlockSpec((None, H_ATT, HD), lambda b, h, sel, pt: (b, 0, 0))
    page_specs = [pl.BlockSpec((None, None, page, H_ATT, HD), page_idx(r)) for r in range(npg)]
    return pl.pallas_call(
        functools.partial(_moba_sample_kernel, past_len=past_len, page=page, ppb=ppb, npg=npg),
        grid_spec=pltpu.PrefetchScalarGridSpec(
            num_scalar_prefetch=2,
            grid=(bsz, H_ATT),
            in_specs=[pl.BlockSpec(memory_space=pltpu.SMEM), tok_spec, tok_spec, tok_spec]
            + page_specs + page_specs
            + [pl.BlockSpec((H_ATT, HD), lambda b, h, sel, pt: (0, 0))],
            out_specs=pl.BlockSpec((None, H_ATT, HD), lambda b, h, sel, pt: (b, 0, 0)),
        ),
        out_shape=jax.ShapeDtypeStruct((bsz, H_ATT, HD), F32),
        compiler_params=_cparams("arbitrary", "arbitrary"),
        name="moba_sample",
    )(sel_flat, pt_flat, slopes, q_heads, kn_heads, vn_heads, *([cache_k] * npg), *([cache_v] * npg),
      g_att.reshape(H_ATT, HD))


def _gla_sample_kernel(q_ref, k_ref, v_ref, r_ref, z_ref, w2_ref, bg_ref, g_ref, s0_ref,
                       o_ref, s_out_ref, *, bsz):
    b = pl.program_id(1)
    a = _log_decay(z_ref[...], w2_ref[...], bg_ref[...])
    live = lax.broadcasted_iota(jnp.int32, (bsz, 1), 0) == b

    def row(x):
        return jnp.sum(jnp.where(live, x, 0.0), axis=0, keepdims=True)

    eye = lax.broadcasted_iota(jnp.int32, (DK, DK), 0) == lax.broadcasted_iota(jnp.int32, (DK, DK), 1)

    def col(x):
        return jnp.sum(jnp.where(eye, jnp.broadcast_to(x, (DK, DK)), 0.0), axis=1, keepdims=True)

    e = jnp.exp(row(a))
    q = row(q_ref[...]) * (DK ** -0.5)
    k = row(k_ref[...])
    v = row(v_ref[...])
    rg = row(r_ref[...])
    s0 = s0_ref[...]
    o = jnp.sum(col(q * e) * s0, axis=0, keepdims=True) + jnp.sum(q * k, axis=1, keepdims=True) * v
    s_out_ref[...] = col(e) * s0 + col(k) * v
    on = _rms(o, g_ref[...]) * (rg * jax.nn.sigmoid(rg))
    o_ref[pl.ds(b, 1), :] = on


def _gla_sample(proj_d, state5, w2p, bg, g_gla, layer, bsz):
    return pl.pallas_call(
        functools.partial(_gla_sample_kernel, bsz=bsz),
        grid=(H_GLA, bsz),
        in_specs=[
            pl.BlockSpec((bsz, DK), lambda h, b: (0, CB_QG + h)),
            pl.BlockSpec((bsz, DK), lambda h, b: (0, CB_KG + h)),
            pl.BlockSpec((bsz, DV), lambda h, b: (0, CB_VG + h)),
            pl.BlockSpec((bsz, DV), lambda h, b: (0, CB_RG + h)),
            pl.BlockSpec((bsz, LANES), lambda h, b: (0, CB_Z)),
            pl.BlockSpec((LANES, DK), lambda h, b: (0, h)),
            pl.BlockSpec((1, DK), lambda h, b: (0, h)),
            pl.BlockSpec((1, DV), lambda h, b: (0, h)),
            pl.BlockSpec((None, None, None, DK, DV), lambda h, b: (layer, b, h, 0, 0)),
        ],
        out_specs=[
            pl.BlockSpec((bsz, DV), lambda h, b: (0, h)),
            pl.BlockSpec((None, None, DK, DV), lambda h, b: (b, h, 0, 0)),
        ],
        out_shape=[jax.ShapeDtypeStruct((bsz, GLA_VW), F32),
                   jax.ShapeDtypeStruct((bsz, H_GLA, DK, DV), F32)],
        compiler_params=_cparams("arbitrary", "arbitrary"),
        name="gla_sample",
    )(proj_d, proj_d, proj_d, proj_d, proj_d, w2p, bg.reshape(1, GLA_KW), g_gla.reshape(1, GLA_VW),
      state5)


def _out_proj_kernel(oa_ref, og_ref, wa_ref, wg_ref, x_ref, g_ref, h_ref, xnt_ref):
    h = x_ref[...] + _mm(oa_ref[...], wa_ref[...]) + _mm(og_ref[...], wg_ref[...])
    h_ref[...] = h
    xnt_ref[...] = _rms(h, g_ref[...]).T.astype(xnt_ref.dtype)


def _out_proj(oa, og, wa, wg, x, g2, tm):
    t, d = x.shape
    return pl.pallas_call(
        _out_proj_kernel,
        grid=(t // tm,),
        in_specs=[pl.BlockSpec((tm, ATT_W), lambda i: (i, 0)),
                  pl.BlockSpec((tm, GLA_VW), lambda i: (i, 0)),
                  pl.BlockSpec((ATT_W, d), lambda i: (0, 0)),
                  pl.BlockSpec((GLA_VW, d), lambda i: (0, 0)),
                  pl.BlockSpec((tm, d), lambda i: (i, 0)),
                  pl.BlockSpec((1, d), lambda i: (0, 0))],
        out_specs=[pl.BlockSpec((tm, d), lambda i: (i, 0)),
                   pl.BlockSpec((d, tm), lambda i: (0, i))],
        out_shape=[jax.ShapeDtypeStruct((t, d), F32), jax.ShapeDtypeStruct((d, t), wa.dtype)],
        compiler_params=_cparams("parallel"),
        name="out_proj",
    )(oa, og, wa, wg, x, g2.reshape(1, d))


def _top_values(s, n):
    vals = []
    for _ in range(n):
        m = jnp.max(s, axis=0, keepdims=True)
        vals.append(m)
        s = jnp.where(s == m, -jnp.inf, s)
    return vals


def _peer_route_kernel(xnt_ref, wq_ref, sk_ref, thr_ref, e1_ref, s2_ref, e2_ref, q_scr):
    q_scr[...] = _mm(wq_ref[...], xnt_ref[...])
    half = PEER_QDIM // 2
    k = PEER_TOPK
    tm = q_scr.shape[1]
    rid = lax.broadcasted_iota(jnp.int32, (SUBLANES, tm), 0)

    def head(h, _):
        r1 = pl.multiple_of(h * PEER_QDIM, PEER_QDIM)
        r2 = pl.multiple_of(h * PEER_QDIM + half, half)
        s1 = jnp.dot(sk_ref[h, 0], q_scr[pl.ds(r1, half), :], precision=HI, preferred_element_type=F32)
        s2 = jnp.dot(sk_ref[h, 1], q_scr[pl.ds(r2, half), :], precision=HI, preferred_element_type=F32)
        v1 = _top_values(s1, k)
        v2 = _top_values(s2, k)
        v1s = jnp.concatenate(v1, axis=0)
        v2s = jnp.concatenate(v2, axis=0)
        parts = [v1[0] + v2s]
        for a in range(1, SUBLANES):
            nb = k // (a + 1)
            parts.append(jnp.where(rid < nb, v1[a] + v2s[:SUBLANES], -jnp.inf))
        parts.append(v1s[SUBLANES:] + v2[0])
        cand = jnp.concatenate(parts, axis=0)
        tau = _top_values(cand, k)[-1]
        smax = v1[0] + v2[0]
        z = jnp.sum(jnp.where(cand >= tau, jnp.exp(cand - smax), 0.0), axis=0, keepdims=True)
        thr = jnp.full(s1.shape, jnp.inf, F32)
        for a in range(k):
            ta = jnp.min(jnp.where(v1[a] + v2s >= tau, v2s, jnp.inf), axis=0, keepdims=True)
            thr = jnp.where(s1 == v1[a], ta, thr)
        rows = pl.ds(pl.multiple_of(h * N_KEYS, N_KEYS), N_KEYS)
        thr_ref[rows, :] = thr
        e1_ref[rows, :] = jnp.exp(s1 - v1[0]) / z
        s2_ref[h] = s2
        e2_ref[h] = jnp.exp(s2 - v2[0])
        return 0

    lax.fori_loop(0, PEER_HEADS, head, 0)


def _peer_route(xnt, wqt, sub_keys, tm):
    d, t = xnt.shape
    nq = wqt.shape[0]
    flat = jax.ShapeDtypeStruct((PEER_HEADS * N_KEYS, t), F32)
    big = jax.ShapeDtypeStruct((PEER_HEADS, N_KEYS, t), F32)
    fspec = pl.BlockSpec((PEER_HEADS * N_KEYS, tm), lambda i: (0, i))
    bspec = pl.BlockSpec((PEER_HEADS, N_KEYS, tm), lambda i: (0, 0, i))
    return pl.pallas_call(
        _peer_route_kernel,
        grid=(t // tm,),
        in_specs=[pl.BlockSpec((d, tm), lambda i: (0, i)),
                  pl.BlockSpec((nq, d), lambda i: (0, 0)),
                  pl.BlockSpec((PEER_HEADS, 2, N_KEYS, PEER_QDIM // 2), lambda i: (0, 0, 0, 0))],
        out_specs=[fspec, fspec, bspec, bspec],
        out_shape=[flat, flat, big, big],
        scratch_shapes=[pltpu.VMEM((nq, tm), F32)],
        compiler_params=_cparams("parallel"),
        name="peer_route",
    )(xnt, wqt, sub_keys)


def _peer_dense_kernel(xnt_ref, u_ref, vt_ref, thr_ref, e1_ref, s2_ref, e2_ref,
                       yt_ref, p_scr, ht_scr, *, te, tm):
    j = pl.program_id(1)

    @pl.when(j == 0)
    def _():
        yt_ref[...] = jnp.zeros_like(yt_ref)

    ht_scr[...] = _mm(u_ref[...], xnt_ref[...])
    n1 = te // N_KEYS
    assert n1 % SUBLANES == 0
    row0 = pl.multiple_of(j * n1, SUBLANES)
    for il in range(n1):
        rs = slice(il * N_KEYS, (il + 1) * N_KEYS)
        for c in range(tm // LANES):
            cs = slice(c * LANES, (c + 1) * LANES)
            w = jnp.zeros((N_KEYS, LANES), F32)
            for h in range(PEER_HEADS):
                row = pl.ds(row0 + (h * N_KEYS + il), 1)
                w = w + jnp.where(s2_ref[h, :, cs] >= thr_ref[row, :][:, cs],
                                  e1_ref[row, :][:, cs] * e2_ref[h, :, cs], 0.0)
            ht = ht_scr[rs, cs]
            act = 0.5 * ht * (1.0 + lax.erf(ht * (2.0 ** -0.5)))
            p_scr[rs, cs] = (w * act).astype(p_scr.dtype)
    yt_ref[...] += _mm(vt_ref[...], p_scr[...])


def _peer_dense(xnt, u_b, vt_b, thr, e1, s2, e2, tm, te):
    d, t = xnt.shape
    ne = u_b.shape[0]
    fspec = pl.BlockSpec((PEER_HEADS * N_KEYS, tm), lambda i, j: (0, i))
    bspec = pl.BlockSpec((PEER_HEADS, N_KEYS, tm), lambda i, j: (0, 0, i))
    return pl.pallas_call(
        functools.partial(_peer_dense_kernel, te=te, tm=tm),
        grid=(t // tm, ne // te),
        in_specs=[pl.BlockSpec((d, tm), lambda i, j: (0, i)),
                  pl.BlockSpec((te, d), lambda i, j: (j, 0)),
                  pl.BlockSpec((d, te), lambda i, j: (0, j)),
                  fspec, fspec, bspec, bspec],
        out_specs=pl.BlockSpec((d, tm), lambda i, j: (0, i)),
        out_shape=jax.ShapeDtypeStruct((d, t), F32),
        scratch_shapes=[pltpu.VMEM((te, tm), u_b.dtype), pltpu.VMEM((te, tm), F32)],
        compiler_params=_cparams("parallel", "arbitrary"),
        name="peer_dense",
    )(xnt, u_b, vt_b, thr, e1, s2, e2)


def _token_tiles(t):
    if t >= 512:
        return 512, 256, 256, 512
    return t, t, t, t


def _layer_weights(l, norm1_g, w_in, w_gate2, b_gate, att_norm_g, gla_norm_g, w_out, norm2_g, w_pq,
                   peer_sub_keys, peer_u, peer_v):
    f32 = dict(
        g1=norm1_g[l],
        w_in=jnp.pad(w_in[l], ((0, 0), (0, N_IN_PAD - N_IN))),
        w2p=jnp.pad(w_gate2[l], ((0, LANES - GATE_RANK), (0, 0))),
        bg=b_gate[l], g_att=att_norm_g[l], g_gla=gla_norm_g[l],
        wo_a=w_out[l, :ATT_W], wo_g=w_out[l, ATT_W:],
        g2=norm2_g[l],
        wqt=w_pq[l].T,
        sub_keys=peer_sub_keys[l],
        u=peer_u[l],
        vt=peer_v[l].T,
    )
    bf16 = dict(f32)
    for name in ("w_in", "wo_a", "wo_g", "wqt", "u", "vt"):
        bf16[name] = f32[name].astype(BF16)
    return f32, bf16


def _peer(h, xnt, w, tm_route, tm_exp):
    route = _peer_route(xnt, w["wqt"], w["sub_keys"], tm_route)
    return _peer_dense(xnt, w["u"], w["vt"], *route, tm_exp, EXPERT_TILE)


def _prompt_layer(x, yt, w, slopes, bsz, seq):
    tm_in, tm_out, tm_route, tm_exp = _token_tiles(bsz * seq)
    x, proj = _norm_matmul(x, yt, w["g1"], w["w_in"], tm_in, 896)
    oa = _moba_prompt(proj, slopes, w["g_att"], bsz, seq)
    og, state = _gla_prompt(proj, w["w2p"], w["bg"], w["g_gla"], bsz, seq)
    h, xnt = _out_proj(oa, og, w["wo_a"], w["wo_g"], x, w["g2"], tm_out)
    return h, _peer(h, xnt, w, tm_route, tm_exp), proj, state


def _sample_layer(x, yt, w, slopes, cache_k, cache_v, state_gla, pt_flat, layer, bsz, n_pages):
    tm_in, tm_out, tm_route, tm_exp = _token_tiles(SAMPLE_PAD)
    x, proj = _norm_matmul(x, yt, w["g1"], w["w_in"], tm_in, 896)
    q_heads = proj[:bsz, :ATT_W].reshape(bsz, H_ATT, HD)
    kn_heads = proj[:bsz, ATT_W:2 * ATT_W].reshape(bsz, H_ATT, HD)
    vn_heads = proj[:bsz, 2 * ATT_W:3 * ATT_W].reshape(bsz, H_ATT, HD)
    ksum = _moba_block_sums(cache_k, pt_flat, layer, bsz, n_pages)
    sel = _moba_select(q_heads, ksum, bsz)
    sel_flat = sel[:, :, :, 0].transpose(0, 2, 1).reshape(-1)
    oa = _moba_sample(q_heads, kn_heads, vn_heads, cache_k, cache_v, sel_flat, pt_flat, slopes, w["g_att"],
                      layer, bsz, n_pages).reshape(bsz, ATT_W)
    og, state = _gla_sample(proj, state_gla, w["w2p"], w["bg"], w["g_gla"], layer, bsz)
    oa = jnp.pad(oa, ((0, SAMPLE_PAD - bsz), (0, 0))).astype(w["wo_a"].dtype)
    og = jnp.pad(og, ((0, SAMPLE_PAD - bsz), (0, 0))).astype(w["wo_a"].dtype)
    h, xnt = _out_proj(oa, og, w["wo_a"], w["wo_g"], x, w["g2"], tm_out)
    return h, _peer(h, xnt, w, tm_route, tm_exp), proj, state


def kernel(x_prompt, x_sample, cache_k, cache_v, state_gla, page_table, norm1_g, w_in, w_gate2, b_gate,
           att_norm_g, gla_norm_g, w_out, norm2_g, w_pq, peer_sub_keys, peer_u, peer_v, final_norm_g):
    bp, seq, d = x_prompt.shape
    bd, sd, _ = x_sample.shape
    depth = w_in.shape[0]
    n_pages = page_table.shape[1]
    assert sd == 1 and d == D_MODEL and seq % MOBA_BLOCK == 0 and bd <= SUBLANES
    tp = bp * seq

    slopes = 2.0 ** (-(8.0 / H_ATT) * jnp.arange(1, H_ATT + 1, dtype=F32))
    pt_flat = page_table.reshape(-1).astype(jnp.int32)

    xp = x_prompt.reshape(tp, d)
    xd = jnp.pad(x_sample.reshape(bd, d), ((0, SAMPLE_PAD - bd), (0, 0)))
    ytp = ytd = None
    kp_l, vp_l, sp_l, kd_l, vd_l, sd_l = [], [], [], [], [], []

    for l in range(depth):
        w_f32, w_bf16 = _layer_weights(l, norm1_g, w_in, w_gate2, b_gate, att_norm_g, gla_norm_g, w_out,
                                       norm2_g, w_pq, peer_sub_keys, peer_u, peer_v)
        xp, ytp, proj_p, s_p = _prompt_layer(xp, ytp, w_bf16, slopes, bp, seq)
        xd, ytd, proj_d, s_d = _sample_layer(xd, ytd, w_f32, slopes, cache_k, cache_v, state_gla, pt_flat,
                                             l, bd, n_pages)
        kp_l.append(proj_p[:, ATT_W:2 * ATT_W].reshape(bp, seq, H_ATT, HD))
        vp_l.append(proj_p[:, 2 * ATT_W:3 * ATT_W].reshape(bp, seq, H_ATT, HD))
        sp_l.append(s_p)
        kd_l.append(proj_d[:bd, ATT_W:2 * ATT_W].reshape(bd, 1, H_ATT, HD))
        vd_l.append(proj_d[:bd, 2 * ATT_W:3 * ATT_W].reshape(bd, 1, H_ATT, HD))
        sd_l.append(s_d)

    y_prompt = _final_norm(xp, ytp, final_norm_g, 512).reshape(bp, seq, d)
    y_sample = _final_norm(xd, ytd, final_norm_g, SAMPLE_PAD)[:bd].reshape(bd, 1, d)
    return (y_prompt, y_sample, jnp.stack(kp_l), jnp.stack(vp_l), jnp.stack(sp_l),
            jnp.stack(kd_l), jnp.stack(vd_l), jnp.stack(sd_l))
```
